```python
import jax
import jax.numpy as jnp
from jax import lax
import numpy as np

D_MODEL = 4096
BATCH = 8
SEQ = 2048
DEPTH = 2

GRID_W = 64
CTX_LEN = 256
Q_BLOCK = 128
ROPE_THETA = 10000.0
NORM_EPS = 1e-6

HEAD_DIM = 128
GQA_HEADS = 16
GQA_KV_HEADS = 4
GQA_GROUP = GQA_HEADS // GQA_KV_HEADS
GQA_Q_W = GQA_HEADS * HEAD_DIM
GQA_KV_W = GQA_KV_HEADS * HEAD_DIM
GQA_SCALE = HEAD_DIM ** -0.5

MLA_HEADS = 8
MLA_Q_LORA = 1024
MLA_KV_LORA = 512
MLA_NOPE_DIM = 128
MLA_ROPE_DIM = 64
MLA_V_DIM = 128
MLA_QK_DIM = MLA_NOPE_DIM + MLA_ROPE_DIM
MLA_OUT_W = MLA_HEADS * MLA_V_DIM
MLA_SCALE = MLA_QK_DIM ** -0.5

FOURIER_GROUPS = 4
FOURIER_GROUP_W = 256
FOURIER_W = FOURIER_GROUPS * FOURIER_GROUP_W

N_BRANCHES = 3

COL_SIZES = (GQA_KV_W, GQA_KV_W, MLA_KV_LORA, MLA_ROPE_DIM, GQA_Q_W, MLA_Q_LORA, FOURIER_W, N_BRANCHES * D_MODEL)
KV_COLS = 2 * GQA_KV_W + MLA_KV_LORA + MLA_ROPE_DIM
IN_COLS = KV_COLS + GQA_Q_W + MLA_Q_LORA + FOURIER_W + N_BRANCHES * D_MODEL

N_EXPERTS = 32
TOP_K = 4
EXPERT_FF = 512
SWIGLU_ALPHA = 1.702
SWIGLU_LIMIT = 7.0

kernel_name = 'hybrid_gqa_mla_fourier_moe_dit_block'


def rmsnorm(x, gain):
    xf = x.astype(jnp.float32)
    y = xf * lax.rsqrt(jnp.mean(xf * xf, axis=-1, keepdims=True) + NORM_EPS)
    return (y * gain.astype(jnp.float32)).astype(x.dtype)


def split_widths(z, widths):
    cuts = []
    total = 0
    for w in widths[:-1]:
        total += w
        cuts.append(total)
    return jnp.split(z, cuts, axis=-1)


def axial_rope_tables(row, col, rot_dim):
    half = rot_dim // 2
    inv = 1.0 / (ROPE_THETA ** (jnp.arange(0, half, 2, dtype=jnp.float32) / half))
    ang_r = row[:, None] * inv[None, :]
    ang_c = col[:, None] * inv[None, :]
    return (jnp.cos(ang_r), jnp.sin(ang_r), jnp.cos(ang_c), jnp.sin(ang_c))


def _rotate(x, cos, sin):
    d2 = x.shape[-1] // 2
    x1, x2 = x[..., :d2], x[..., d2:]
    c = cos[:, None, :]
    s = sin[:, None, :]
    return jnp.concatenate([x1 * c - x2 * s, x1 * s + x2 * c], axis=-1)


def apply_axial_rope(x, tables):
    cos_r, sin_r, cos_c, sin_c = tables
    xf = x.astype(jnp.float32)
    half = x.shape[-1] // 2
    out = jnp.concatenate([_rotate(xf[..., :half], cos_r, sin_r), _rotate(xf[..., half:], cos_c, sin_c)], axis=-1)
    return out.astype(x.dtype)


def blocked_attention(q, k, v, scale):
    b, nq, hk, g, dq = q.shape
    qb = jnp.moveaxis(q.reshape(b, nq // Q_BLOCK, Q_BLOCK, hk, g, dq), 1, 0)

    def one_block(qi):
        s = jnp.einsum('bqhgd,bshd->bhgqs', qi, k).astype(jnp.float32) * scale
        p = jax.nn.softmax(s, axis=-1).astype(v.dtype)
        return jnp.einsum('bhgqs,bshd->bqhgd', p, v)

    o = jnp.moveaxis(lax.map(one_block, qb), 0, 1)
    return o.reshape(b, nq, -1)


def gqa_q(zq, q_norm, rope):
    b, n, _ = zq.shape
    q = rmsnorm(zq.reshape(b, n, GQA_HEADS, HEAD_DIM), q_norm)
    if rope is not None:
        q = apply_axial_rope(q, rope)
    return q.reshape(b, n, GQA_KV_HEADS, GQA_GROUP, HEAD_DIM)


def gqa_kv(zk, zv, k_norm, rope):
    b, n, _ = zk.shape
    k = rmsnorm(zk.reshape(b, n, GQA_KV_HEADS, HEAD_DIM), k_norm)
    if rope is not None:
        k = apply_axial_rope(k, rope)
    return k, zv.reshape(b, n, GQA_KV_HEADS, HEAD_DIM)


def mla_q(zcq, q_norm, w_uq, rope):
    b, n, _ = zcq.shape
    q = (rmsnorm(zcq, q_norm) @ w_uq).reshape(b, n, MLA_HEADS, MLA_QK_DIM)
    q_nope, q_rope = q[..., :MLA_NOPE_DIM], q[..., MLA_NOPE_DIM:]
    if rope is not None:
        q_rope = apply_axial_rope(q_rope, rope)
    return jnp.concatenate([q_nope, q_rope], axis=-1)[:, :, :, None, :]


def mla_kv(zckv, zkr, kv_norm, w_ukv, rope):
    b, n, _ = zckv.shape
    kv = (rmsnorm(zckv, kv_norm) @ w_ukv).reshape(b, n, MLA_HEADS, MLA_NOPE_DIM + MLA_V_DIM)
    k_nope, v = kv[..., :MLA_NOPE_DIM], kv[..., MLA_NOPE_DIM:]
    k_rope = zkr[:, :, None, :]
    if rope is not None:
        k_rope = apply_axial_rope(k_rope, rope)
    k_rope = jnp.broadcast_to(k_rope, (b, n, MLA_HEADS, MLA_ROPE_DIM))
    return jnp.concatenate([k_nope, k_rope], axis=-1), v


def fourier_mix(zf):
    b, n, _ = zf.shape
    zg = zf.astype(jnp.float32).reshape(b, n, FOURIER_GROUPS, FOURIER_GROUP_W)
    y = jnp.fft.fft2(zg, axes=(1, 3), norm='ortho').real
    return y.reshape(b, n, FOURIER_W).astype(zf.dtype)


def merge_branches(y_gqa, y_mla, y_fft, z_gate, w_br_gqa, w_br_mla, w_br_fourier, w_out):
    g_gqa, g_mla, g_fft = jnp.split(jax.nn.sigmoid(z_gate), N_BRANCHES, axis=-1)
    m = g_gqa * (y_gqa @ w_br_gqa) + g_mla * (y_mla @ w_br_mla) + g_fft * (y_fft @ w_br_fourier)
    return m @ w_out


def clamped_swiglu(hgu):
    g = jnp.minimum(hgu[..., 0::2], SWIGLU_LIMIT)
    lin = jnp.clip(hgu[..., 1::2], -SWIGLU_LIMIT, SWIGLU_LIMIT)
    return g * jax.nn.sigmoid(SWIGLU_ALPHA * g) * (lin + 1.0)


def moe(t, w_router, b_router, w_gate_up, b_gate_up, w_down, b_down):
    logits = (t @ w_router + b_router).astype(jnp.float32)
    top_val, top_idx = lax.top_k(logits, TOP_K)
    top_w = jax.nn.softmax(top_val, axis=-1)
    comb = jnp.sum(jax.nn.one_hot(top_idx, N_EXPERTS, dtype=jnp.float32) * top_w[..., None], axis=1).astype(t.dtype)
    out = jnp.zeros_like(t)
    for e in range(N_EXPERTS):
        y = clamped_swiglu(t @ w_gate_up[e] + b_gate_up[e]) @ w_down[e] + b_down[e]
        out = out + comb[:, e:e + 1] * y
    return out


def trunk_layer(h, hc, mod, mod_c, rope_gqa, rope_mla, norm_mix, w_in, gqa_q_norm, gqa_k_norm,
                mla_q_norm, mla_w_uq, mla_kv_norm, mla_w_ukv, w_br_gqa, w_br_mla, w_br_fourier,
                w_out, norm_ffn, w_router, b_router, w_gate_up, b_gate_up, w_down, b_down, last):
    sh1, sc1, g1, sh2, sc2, g2 = jnp.split(mod[:, None, :], 6, axis=-1)
    sh1c, sc1c, g1c, sh2c, sc2c, g2c = jnp.split(mod_c[:, None, :], 6, axis=-1)

    u = rmsnorm(h, norm_mix) * (1.0 + sc1) + sh1
    uc = rmsnorm(hc, norm_mix) * (1.0 + sc1c) + sh1c
    z = u @ w_in
    zc = uc @ (w_in[:, :KV_COLS] if last else w_in)
    zk_a, zv_a, zckv, zkr, zq_a, zcq, zf, zg = split_widths(z, COL_SIZES)
    zk_ac, zv_ac, zckv_c, zkr_c = split_widths(zc[..., :KV_COLS], COL_SIZES[:4])

    k_a_lat, v_a_lat = gqa_kv(zk_a, zv_a, gqa_k_norm, rope_gqa)
    k_a_ctx, v_a_ctx = gqa_kv(zk_ac, zv_ac, gqa_k_norm, None)
    y_a = blocked_attention(gqa_q(zq_a, gqa_q_norm, rope_gqa),
                            jnp.concatenate([k_a_ctx, k_a_lat], axis=1),
                            jnp.concatenate([v_a_ctx, v_a_lat], axis=1), GQA_SCALE)

    k_b_lat, v_b_lat = mla_kv(zckv, zkr, mla_kv_norm, mla_w_ukv, rope_mla)
    k_b_ctx, v_b_ctx = mla_kv(zckv_c, zkr_c, mla_kv_norm, mla_w_ukv, None)
    y_b = blocked_attention(mla_q(zcq, mla_q_norm, mla_w_uq, rope_mla),
                            jnp.concatenate([k_b_ctx, k_b_lat], axis=1),
                            jnp.concatenate([v_b_ctx, v_b_lat], axis=1), MLA_SCALE)

    y_c = fourier_mix(zf)

    h = h + g1 * merge_branches(y_a, y_b, y_c, zg, w_br_gqa, w_br_mla, w_br_fourier, w_out)

    if not last:
        _, _, _, _, zq_ac, zcq_c, zf_c, zg_c = split_widths(zc, COL_SIZES)
        yc_a = blocked_attention(gqa_q(zq_ac, gqa_q_norm, None), k_a_ctx, v_a_ctx, GQA_SCALE)
        yc_b = blocked_attention(mla_q(zcq_c, mla_q_norm, mla_w_uq, None), k_b_ctx, v_b_ctx, MLA_SCALE)
        yc_c = fourier_mix(zf_c)
        hc = hc + g1c * merge_branches(yc_a, yc_b, yc_c, zg_c, w_br_gqa, w_br_mla, w_br_fourier, w_out)

    b, n, d = h.shape
    v = rmsnorm(h, norm_ffn) * (1.0 + sc2) + sh2
    if last:
        f = moe(v.reshape(b * n, d), w_router, b_router, w_gate_up, b_gate_up, w_down, b_down)
        h = h + g2 * f.reshape(b, n, d)
    else:
        vc = rmsnorm(hc, norm_ffn) * (1.0 + sc2c) + sh2c
        tokens = jnp.concatenate([v.reshape(b * n, d), vc.reshape(-1, d)], axis=0)
        f = moe(tokens, w_router, b_router, w_gate_up, b_gate_up, w_down, b_down)
        h = h + g2 * f[:b * n].reshape(b, n, d)
        hc = hc + g2c * f[b * n:].reshape(hc.shape)
    return h, hc


def setup_inputs(seed: int = 0) -> dict:
    key = jax.random.key(seed)
    k = jax.random.split(key, 26)
    f32 = jnp.float32

    def normal(kk, shape, std):
        return jax.random.normal(kk, shape, f32) * std

    def gain(kk, shape):
        return 1.0 + 0.02 * jax.random.normal(kk, shape, f32)

    D = D_MODEL
    return {
        'x': normal(k[0], (BATCH, SEQ, D), 1.0),
        'c': normal(k[1], (BATCH, D), 1.0),
        'ctx': normal(k[2], (BATCH, CTX_LEN, D), 1.0),
        'c_ctx': normal(k[3], (D,), 1.0),
        'w_ada': normal(k[4], (DEPTH, D, 6 * D), 0.5 * D ** -0.5),
        'b_ada': normal(k[5], (DEPTH, 6 * D), 0.02),
        'norm_mix': gain(k[6], (DEPTH, D)),
        'w_in': normal(k[7], (DEPTH, D, IN_COLS), D ** -0.5),
        'gqa_q_norm': gain(k[8], (DEPTH, HEAD_DIM)),
        'gqa_k_norm': gain(k[9], (DEPTH, HEAD_DIM)),
        'mla_q_norm': gain(k[10], (DEPTH, MLA_Q_LORA)),
        'mla_w_uq': normal(k[11], (DEPTH, MLA_Q_LORA, MLA_HEADS * MLA_QK_DIM), MLA_Q_LORA ** -0.5),
        'mla_kv_norm': gain(k[12], (DEPTH, MLA_KV_LORA)),
        'mla_w_ukv': normal(k[13], (DEPTH, MLA_KV_LORA, MLA_HEADS * (MLA_NOPE_DIM + MLA_V_DIM)), MLA_KV_LORA ** -0.5),
        'w_br_gqa': normal(k[14], (DEPTH, GQA_Q_W, D), GQA_Q_W ** -0.5),
        'w_br_mla': normal(k[15], (DEPTH, MLA_OUT_W, D), MLA_OUT_W ** -0.5),
        'w_br_fourier': normal(k[16], (DEPTH, FOURIER_W, D), FOURIER_W ** -0.5),
        'w_out': normal(k[17], (DEPTH, D, D), D ** -0.5),
        'norm_ffn': gain(k[18], (DEPTH, D)),
        'w_router': normal(k[19], (DEPTH, D, N_EXPERTS), D ** -0.5),
        'b_router': normal(k[20], (DEPTH, N_EXPERTS), 0.01),
        'w_gate_up': normal(k[21], (DEPTH, N_EXPERTS, D, 2 * EXPERT_FF), D ** -0.5),
        'b_gate_up': normal(k[22], (DEPTH, N_EXPERTS, 2 * EXPERT_FF), 0.01),
        'w_down': normal(k[23], (DEPTH, N_EXPERTS, EXPERT_FF, D), EXPERT_FF ** -0.5),
        'b_down': normal(k[24], (DEPTH, N_EXPERTS, D), 0.01),
        'norm_final': gain(k[25], (D,)),
    }


def reference(x, c, ctx, c_ctx, w_ada, b_ada, norm_mix, w_in, gqa_q_norm, gqa_k_norm, mla_q_norm,
              mla_w_uq, mla_kv_norm, mla_w_ukv, w_br_gqa, w_br_mla, w_br_fourier, w_out, norm_ffn,
              w_router, b_router, w_gate_up, b_gate_up, w_down, b_down, norm_final):
    ROWS = x.shape[1] // GRID_W
    row = jnp.repeat(jnp.arange(ROWS, dtype=jnp.float32), GRID_W)
    col = jnp.tile(jnp.arange(GRID_W, dtype=jnp.float32), ROWS)
    rope_gqa = axial_rope_tables(row, col, HEAD_DIM)
    rope_mla = axial_rope_tables(row, col, MLA_ROPE_DIM)
    c_act = jax.nn.silu(c)
    c_ctx_act = jax.nn.silu(c_ctx)[None, :]
    h, hc = x, ctx
    for l in range(DEPTH):
        mod = c_act @ w_ada[l] + b_ada[l]
        mod_c = c_ctx_act @ w_ada[l] + b_ada[l]
        h, hc = trunk_layer(h, hc, mod, mod_c, rope_gqa, rope_mla, norm_mix[l], w_in[l],
                            gqa_q_norm[l], gqa_k_norm[l], mla_q_norm[l], mla_w_uq[l], mla_kv_norm[l],
                            mla_w_ukv[l], w_br_gqa[l], w_br_mla[l], w_br_fourier[l], w_out[l],
                            norm_ffn[l], w_router[l], b_router[l], w_gate_up[l], b_gate_up[l],
                            w_down[l], b_down[l], l == DEPTH - 1)
    return rmsnorm(h, norm_final)
```

```python
import functools
import math

import jax
import jax.numpy as jnp
from jax import lax
from jax.experimental import pallas as pl
from jax.experimental.pallas import tpu as pltpu

F32 = jnp.float32
BF16 = jnp.bfloat16
U32 = jnp.uint32

GRID_W = 64
ROPE_THETA = 10000.0
NORM_EPS = 1e-6
HEAD_DIM = 128
GQA_HEADS = 16
GQA_KV_HEADS = 4
GQA_GROUP = GQA_HEADS // GQA_KV_HEADS
GQA_Q_W = GQA_HEADS * HEAD_DIM
GQA_KV_W = GQA_KV_HEADS * HEAD_DIM
GQA_SCALE = HEAD_DIM ** -0.5
MLA_HEADS = 8
MLA_Q_LORA = 1024
MLA_KV_LORA = 512
MLA_NOPE_DIM = 128
MLA_ROPE_DIM = 64
MLA_V_DIM = 128
MLA_QK_DIM = MLA_NOPE_DIM + MLA_ROPE_DIM
MLA_OUT_W = MLA_HEADS * MLA_V_DIM
MLA_SCALE = MLA_QK_DIM ** -0.5
FOURIER_GROUPS = 4
FOURIER_GROUP_W = 256
FOURIER_W = FOURIER_GROUPS * FOURIER_GROUP_W
TOP_K = 4
SWIGLU_ALPHA = 1.702
SWIGLU_LIMIT = 7.0

LANES = 128
V7X_VMEM_BYTES = 64 * 2 ** 20
VMEM_CAP = V7X_VMEM_BYTES - 6 * 2 ** 20
VMEM_FLOOR = 32 * 2 ** 20
VMEM_INTERNAL = 8 * 2 ** 20

KV_W = 2048
KV_OFF_K, KV_OFF_V, KV_OFF_CKV, KV_OFF_KR = 0, 512, 1024, 1536
MAIN_OFF_QA, MAIN_OFF_CQ, MAIN_OFF_F, MAIN_OFF_G = 0, 2048, 3072, 4096
NEG_BIG = -1e30


def _vmem(block_bytes, temp_bytes=0):
    return int(min(VMEM_CAP, max(VMEM_FLOOR, 2 * block_bytes + temp_bytes + VMEM_INTERNAL)))


def _params(sem, block_bytes, temp_bytes=0):
    return pltpu.CompilerParams(dimension_semantics=sem, vmem_limit_bytes=_vmem(block_bytes, temp_bytes))


def _nbytes(shape, dtype):
    return math.prod(shape) * jnp.dtype(dtype).itemsize


def _pick(n, target):
    if n <= target:
        return n
    for cand in range(target, 7, -1):
        if n % cand == 0 and cand % 8 == 0:
            return cand
    return n


def _rms(x, gain):
    return x * lax.rsqrt(jnp.mean(x * x, axis=-1, keepdims=True) + NORM_EPS) * gain


def _rope(x, cos, sin, blk):
    lane = lax.broadcasted_iota(jnp.int32, x.shape, x.ndim - 1)
    first = (lane % (2 * blk)) < blk
    swapped = jnp.where(first, pltpu.roll(x, LANES - blk, x.ndim - 1), pltpu.roll(x, blk, x.ndim - 1))
    return x * cos + swapped * sin


def _pack_bf16_pair(lo, hi):
    lo_bits = lax.bitcast_convert_type(lo.astype(BF16).astype(F32), U32) >> 16
    hi_bits = lax.bitcast_convert_type(hi.astype(BF16).astype(F32), U32) & jnp.uint32(0xFFFF0000)
    return lo_bits | hi_bits


def _unpack_bf16_pair(word):
    lo = lax.bitcast_convert_type(word << 16, F32)
    hi = lax.bitcast_convert_type(word & jnp.uint32(0xFFFF0000), F32)
    return lo, hi


def _dot(a, b):
    return jnp.dot(a, b, preferred_element_type=F32)


def _dot_nt(a, b):
    return lax.dot_general(a, b, (((1,), (1,)), ((), ())), preferred_element_type=F32)


def _ada_kernel(c_ref, w_ref, b_ref, o_ref):
    c = c_ref[...]
    act = (c * jax.nn.sigmoid(c)).astype(BF16)
    o_ref[0] = _dot(act, w_ref[0].astype(BF16)) + b_ref[0]


def _ada_mod(c_all, w_ada, b_ada):
    depth, d, n6 = w_ada.shape
    rows = c_all.shape[0]
    bn = _pick(n6, 512)
    blocks = _nbytes((rows, d), F32) + _nbytes((d, bn), F32) + _nbytes((rows, bn), F32) * 2
    return pl.pallas_call(
        _ada_kernel,
        grid=(depth, n6 // bn),
        in_specs=[
            pl.BlockSpec((rows, d), lambda l, j: (0, 0)),
            pl.BlockSpec((1, d, bn), lambda l, j: (l, 0, j)),
            pl.BlockSpec((1, 1, bn), lambda l, j: (l, 0, j)),
        ],
        out_specs=pl.BlockSpec((1, rows, bn), lambda l, j: (l, 0, j)),
        out_shape=jax.ShapeDtypeStruct((depth, rows, n6), F32),
        compiler_params=_params(("arbitrary", "arbitrary"), blocks, _nbytes((d, bn), BF16)),
        name="adaln_mod",
    )(c_all, w_ada, b_ada.reshape(depth, 1, n6))


def _norm_mod_kernel(x_ref, g_ref, sh_ref, sc_ref, o_ref):
    y = _rms(x_ref[0], g_ref[...])
    o_ref[0] = (y * (1.0 + sc_ref[0]) + sh_ref[0]).astype(o_ref.dtype)


def _norm_mod(h, gain, mod3, sh_blk, sc_blk):
    nb, n, d = h.shape
    bn = _pick(n, 256)
    blocks = _nbytes((bn, d), F32) + _nbytes((bn, d), BF16) + 3 * _nbytes((1, d), F32)
    return pl.pallas_call(
        _norm_mod_kernel,
        grid=(nb, n // bn),
        in_specs=[
            pl.BlockSpec((1, bn, d), lambda b, i: (b, i, 0)),
            pl.BlockSpec((1, d), lambda b, i: (0, 0)),
            pl.BlockSpec((1, 1, d), lambda b, i: (b, 0, sh_blk)),
            pl.BlockSpec((1, 1, d), lambda b, i: (b, 0, sc_blk)),
        ],
        out_specs=pl.BlockSpec((1, bn, d), lambda b, i: (b, i, 0)),
        out_shape=jax.ShapeDtypeStruct((nb, n, d), BF16),
        compiler_params=_params(("arbitrary", "arbitrary"), blocks, 2 * _nbytes((bn, d), F32)),
        name="norm_mod",
    )(h, gain.reshape(1, d), mod3, mod3)


def _mm_kernel(x_ref, w_ref, o_ref):
    o_ref[...] = _dot(x_ref[...], w_ref[...]).astype(o_ref.dtype)


def _matmul(x2, w, rows, out_dtype=BF16):
    k = x2.shape[1]
    n = w.shape[1]
    bm = _pick(rows, 1024)
    bn = _pick(n, 1024)
    blocks = _nbytes((bm, k), BF16) + _nbytes((k, bn), BF16) + _nbytes((bm, bn), out_dtype)
    return pl.pallas_call(
        _mm_kernel,
        grid=(rows // bm, n // bn),
        in_specs=[
            pl.BlockSpec((bm, k), lambda i, j: (i, 0)),
            pl.BlockSpec((k, bn), lambda i, j: (0, j)),
        ],
        out_specs=pl.BlockSpec((bm, bn), lambda i, j: (i, j)),
        out_shape=jax.ShapeDtypeStruct((rows, n), out_dtype),
        compiler_params=_params(("arbitrary", "arbitrary"), blocks, _nbytes((bm, bn), F32)),
        name="matmul",
    )(x2, w)


def _prep_kv_kernel(z_ref, gk_ref, gkv_ref, wukv_ref, cg_ref, sg_ref, cm_ref, sm_ref, ka_ref, kvb_ref, kr_ref):
    gk = gk_ref[...]
    cg, sg = cg_ref[0], sg_ref[0]
    for h in range(GQA_KV_HEADS):
        sl = slice(KV_OFF_K + h * HEAD_DIM, KV_OFF_K + (h + 1) * HEAD_DIM)
        y = _rope(_rms(z_ref[0, :, sl].astype(F32), gk), cg, sg, HEAD_DIM // 4)
        ka_ref[0, :, h * HEAD_DIM:(h + 1) * HEAD_DIM] = y.astype(ka_ref.dtype)
    ckv = _rms(z_ref[0, :, KV_OFF_CKV:KV_OFF_CKV + MLA_KV_LORA].astype(F32), gkv_ref[...])
    kvb_ref[0] = _dot(ckv.astype(BF16), wukv_ref[...]).astype(kvb_ref.dtype)
    kr = z_ref[0, :, KV_OFF_KR:KV_OFF_KR + LANES].astype(F32)
    kr_ref[0] = _rope(kr, cm_ref[0], sm_ref[0], MLA_ROPE_DIM // 4).astype(kr_ref.dtype)


def _prep_kv(z_kv, gk, gkv, wukv, tabs, n_lat):
    nb, n, _ = z_kv.shape
    bn = _pick(n, 512)
    kvw = wukv.shape[1]
    tab = pl.BlockSpec((1, bn, LANES), lambda b, i: (b // n_lat, i, 0))
    blocks = (_nbytes((bn, KV_W), BF16) + _nbytes(wukv.shape, BF16) + 4 * _nbytes((bn, LANES), F32)
              + _nbytes((bn, GQA_KV_W + kvw + LANES), BF16))
    return pl.pallas_call(
        _prep_kv_kernel,
        grid=(nb, n // bn),
        in_specs=[
            pl.BlockSpec((1, bn, KV_W), lambda b, i: (b, i, 0)),
            pl.BlockSpec((1, HEAD_DIM), lambda b, i: (0, 0)),
            pl.BlockSpec((1, MLA_KV_LORA), lambda b, i: (0, 0)),
            pl.BlockSpec(wukv.shape, lambda b, i: (0, 0)),
            tab, tab, tab, tab,
        ],
        out_specs=[
            pl.BlockSpec((1, bn, GQA_KV_W), lambda b, i: (b, i, 0)),
            pl.BlockSpec((1, bn, kvw), lambda b, i: (b, i, 0)),
            pl.BlockSpec((1, bn, LANES), lambda b, i: (b, i, 0)),
        ],
        out_shape=[
            jax.ShapeDtypeStruct((nb, n, GQA_KV_W), BF16),
            jax.ShapeDtypeStruct((nb, n, kvw), BF16),
            jax.ShapeDtypeStruct((nb, n, LANES), BF16),
        ],
        compiler_params=_params(("arbitrary", "arbitrary"), blocks, 4 * _nbytes((bn, kvw), F32)),
        name="prep_kv",
    )(z_kv, gk.reshape(1, HEAD_DIM), gkv.reshape(1, MLA_KV_LORA), wukv, *tabs)


def _prep_q_kernel(zq_ref, zcq_ref, gq_ref, gcq_ref, wuq_ref, cg_ref, sg_ref, cm_ref, sm_ref, qa_ref, qb_ref):
    gq = gq_ref[...]
    cg, sg = cg_ref[0], sg_ref[0]
    for h in range(GQA_HEADS):
        sl = slice(h * HEAD_DIM, (h + 1) * HEAD_DIM)
        y = _rope(_rms(zq_ref[0, :, sl].astype(F32), gq), cg, sg, HEAD_DIM // 4)
        qa_ref[0, :, sl] = (y * GQA_SCALE).astype(qa_ref.dtype)
    cq = _rms(zcq_ref[0].astype(F32), gcq_ref[...])
    q = _dot(cq.astype(BF16), wuq_ref[...])
    nope_w = MLA_HEADS * MLA_NOPE_DIM
    qb_ref[0, :, :nope_w] = (q[:, :nope_w] * MLA_SCALE).astype(qb_ref.dtype)
    cm, sm = cm_ref[0], sm_ref[0]
    for h in range(MLA_HEADS):
        sl = slice(nope_w + h * LANES, nope_w + (h + 1) * LANES)
        qb_ref[0, :, sl] = (_rope(q[:, sl], cm, sm, MLA_ROPE_DIM // 4) * MLA_SCALE).astype(qb_ref.dtype)


def _prep_q(z_main, nb, gq, gcq, wuq, tabs, n_lat):
    n = z_main.shape[1]
    bn = _pick(n, 256)
    qbw = wuq.shape[1]
    tab = pl.BlockSpec((1, bn, LANES), lambda b, i: (b // n_lat, i, 0))
    blocks = (_nbytes((bn, GQA_Q_W + MLA_Q_LORA), BF16) + _nbytes(wuq.shape, BF16)
              + 4 * _nbytes((bn, LANES), F32) + _nbytes((bn, GQA_Q_W + qbw), BF16))
    return pl.pallas_call(
        _prep_q_kernel,
        grid=(nb, n // bn),
        in_specs=[
            pl.BlockSpec((1, bn, GQA_Q_W), lambda b, i: (b, i, MAIN_OFF_QA // GQA_Q_W)),
            pl.BlockSpec((1, bn, MLA_Q_LORA), lambda b, i: (b, i, MAIN_OFF_CQ // MLA_Q_LORA)),
            pl.BlockSpec((1, HEAD_DIM), lambda b, i: (0, 0)),
            pl.BlockSpec((1, MLA_Q_LORA), lambda b, i: (0, 0)),
            pl.BlockSpec(wuq.shape, lambda b, i: (0, 0)),
            tab, tab, tab, tab,
        ],
        out_specs=[
            pl.BlockSpec((1, bn, GQA_Q_W), lambda b, i: (b, i, 0)),
            pl.BlockSpec((1, bn, qbw), lambda b, i: (b, i, 0)),
        ],
        out_shape=[
            jax.ShapeDtypeStruct((nb, n, GQA_Q_W), BF16),
            jax.ShapeDtypeStruct((nb, n, qbw), BF16),
        ],
        compiler_params=_params(("arbitrary", "arbitrary"), blocks, 4 * _nbytes((bn, qbw), F32)),
        name="prep_q",
    )(z_main, z_main, gq.reshape(1, HEAD_DIM), gcq.reshape(1, MLA_Q_LORA), wuq, *tabs)


def _softmax_pv(scores, values):
    m = scores[0].max(axis=-1, keepdims=True)
    for s in scores[1:]:
        m = jnp.maximum(m, s.max(axis=-1, keepdims=True))
    acc = None
    den = None
    for s, v in zip(scores, values):
        p = jnp.exp(s - m)
        l = p.sum(axis=-1, keepdims=True)
        o = _dot(p.astype(v.dtype), v)
        acc = o if acc is None else acc + o
        den = l if den is None else den + l
    return acc / den


def _gqa_kernel(q_ref, kc_ref, kl_ref, vc_ref, vl_ref, o_ref, *, n_lat, has_ctx_batch):
    bq = q_ref.shape[1]
    q = q_ref[0]
    qs = jnp.concatenate([q[:, g * HEAD_DIM:(g + 1) * HEAD_DIM] for g in range(GQA_GROUP)], axis=0)

    def finish(scores, values):
        o = _softmax_pv(scores, values)
        for g in range(GQA_GROUP):
            o_ref[0, :, g * HEAD_DIM:(g + 1) * HEAD_DIM] = o[g * bq:(g + 1) * bq].astype(o_ref.dtype)

    def latent():
        finish([_dot_nt(qs, kc_ref[0]), _dot_nt(qs, kl_ref[0])], [vc_ref[0], vl_ref[0]])

    def context():
        finish([_dot_nt(qs, kc_ref[0])], [vc_ref[0]])

    if has_ctx_batch:
        b = pl.program_id(0)
        pl.when(b < n_lat)(latent)
        pl.when(b == n_lat)(context)
    else:
        latent()


def _mla_kernel(qn_ref, qr_ref, knc_ref, krc_ref, knl_ref, krl_ref, vc_ref, vl_ref, o_ref, *, n_lat, has_ctx_batch):
    q = jnp.concatenate([qn_ref[0], qr_ref[0]], axis=1)

    def finish(scores, values):
        o_ref[0] = _softmax_pv(scores, values).astype(o_ref.dtype)

    def latent():
        kc = jnp.concatenate([knc_ref[0], krc_ref[0]], axis=1)
        kl = jnp.concatenate([knl_ref[0], krl_ref[0]], axis=1)
        finish([_dot_nt(q, kc), _dot_nt(q, kl)], [vc_ref[0], vl_ref[0]])

    def context():
        kc = jnp.concatenate([knc_ref[0], krc_ref[0]], axis=1)
        finish([_dot_nt(q, kc)], [vc_ref[0]])

    if has_ctx_batch:
        b = pl.program_id(0)
        pl.when(b < n_lat)(latent)
        pl.when(b == n_lat)(context)
    else:
        latent()


def _ctx_idx(b, i, n_lat):
    return jnp.where(b < n_lat, b, i)


def _lat_idx(b, n_lat):
    return jnp.where(b < n_lat, b, 0)


def _gqa_attention(qa, ka, z_kv, nb, n_lat, ctx_len):
    n = qa.shape[1]
    bq = ctx_len
    qw = GQA_GROUP * HEAD_DIM
    v_blk = KV_OFF_V // HEAD_DIM
    kern = functools.partial(_gqa_kernel, n_lat=n_lat, has_ctx_batch=nb > n_lat)
    blocks = (2 * _nbytes((bq, qw), BF16) + 2 * _nbytes((ctx_len + n, HEAD_DIM), BF16))
    temps = 3 * _nbytes((GQA_GROUP * bq, ctx_len + n), F32)
    return pl.pallas_call(
        kern,
        grid=(nb, GQA_KV_HEADS, n // bq),
        in_specs=[
            pl.BlockSpec((1, bq, qw), lambda b, h, i: (b, i, h)),
            pl.BlockSpec((1, ctx_len, HEAD_DIM), lambda b, h, i: (n_lat, _ctx_idx(b, i, n_lat), h)),
            pl.BlockSpec((1, n, HEAD_DIM), lambda b, h, i: (_lat_idx(b, n_lat), 0, h)),
            pl.BlockSpec((1, ctx_len, HEAD_DIM), lambda b, h, i: (n_lat, _ctx_idx(b, i, n_lat), v_blk + h)),
            pl.BlockSpec((1, n, HEAD_DIM), lambda b, h, i: (_lat_idx(b, n_lat), 0, v_blk + h)),
        ],
        out_specs=pl.BlockSpec((1, bq, qw), lambda b, h, i: (b, i, h)),
        out_shape=jax.ShapeDtypeStruct((nb, n, GQA_Q_W), BF16),
        compiler_params=_params(("arbitrary", "arbitrary", "arbitrary"), blocks, temps),
        name="gqa_attention",
    )(qa, ka, ka, z_kv, z_kv)


def _mla_attention(qb, kvb, kr, nb, n_lat, ctx_len):
    n = qb.shape[1]
    bq = ctx_len
    kern = functools.partial(_mla_kernel, n_lat=n_lat, has_ctx_batch=nb > n_lat)
    blocks = 3 * _nbytes((bq, LANES), BF16) + 3 * _nbytes((ctx_len + n, LANES), BF16)
    temps = 3 * _nbytes((bq, ctx_len + n), F32) + 2 * _nbytes((ctx_len + n, 2 * LANES), BF16)
    ctx_spec = lambda col: pl.BlockSpec((1, ctx_len, LANES), lambda b, h, i: (n_lat, _ctx_idx(b, i, n_lat), col(h)))
    lat_spec = lambda col: pl.BlockSpec((1, n, LANES), lambda b, h, i: (_lat_idx(b, n_lat), 0, col(h)))
    return pl.pallas_call(
        kern,
        grid=(nb, MLA_HEADS, n // bq),
        in_specs=[
            pl.BlockSpec((1, bq, LANES), lambda b, h, i: (b, i, h)),
            pl.BlockSpec((1, bq, LANES), lambda b, h, i: (b, i, MLA_HEADS + h)),
            ctx_spec(lambda h: h), ctx_spec(lambda h: 0),
            lat_spec(lambda h: h), lat_spec(lambda h: 0),
            ctx_spec(lambda h: MLA_HEADS + h), lat_spec(lambda h: MLA_HEADS + h),
        ],
        out_specs=pl.BlockSpec((1, bq, LANES), lambda b, h, i: (b, i, h)),
        out_shape=jax.ShapeDtypeStruct((nb, n, MLA_OUT_W), BF16),
        compiler_params=_params(("arbitrary", "arbitrary", "arbitrary"), blocks, temps),
        name="mla_attention",
    )(qb, qb, kvb, kr, kvb, kr, kvb, kvb)


def _fourier_kernel(x_ref, wc_ref, wp_ref, o_ref):
    gw = x_ref.shape[2]
    xcs = _dot(x_ref[0], wc_ref[...]).astype(BF16)
    stacked = jnp.concatenate([xcs[:, :gw], xcs[:, gw:]], axis=0)
    o_ref[0] = _dot(wp_ref[0], stacked).astype(o_ref.dtype)


def _fourier(z_main, nb, n_lat, w_chan, w_pos):
    n = z_main.shape[1]
    gw = FOURIER_GROUP_W
    blocks = 2 * _nbytes((n, gw), BF16) + _nbytes((gw, 2 * gw), BF16) + _nbytes((n, 2 * n), BF16)
    temps = _nbytes((n, 2 * gw), F32) + 2 * _nbytes((2 * n, gw), BF16) + _nbytes((n, gw), F32)
    return pl.pallas_call(
        _fourier_kernel,
        grid=(nb, FOURIER_GROUPS),
        in_specs=[
            pl.BlockSpec((1, n, gw), lambda b, g: (b, 0, MAIN_OFF_F // gw + g)),
            pl.BlockSpec((gw, 2 * gw), lambda b, g: (0, 0)),
            pl.BlockSpec((1, n, 2 * n), lambda b, g: (b // n_lat, 0, 0)),
        ],
        out_specs=pl.BlockSpec((1, n, gw), lambda b, g: (b, 0, g)),
        out_shape=jax.ShapeDtypeStruct((nb, n, FOURIER_W), BF16),
        compiler_params=_params(("arbitrary", "arbitrary"), blocks, temps),
        name="fourier_mix",
    )(z_main, w_chan, w_pos)


def _merge_kernel(ya_ref, yb_ref, yc_ref, wa_ref, wb_ref, wc_ref, ga_ref, gb_ref, gc_ref, o_ref):
    m = jax.nn.sigmoid(ga_ref[...].astype(F32)) * _dot(ya_ref[...], wa_ref[...])
    m = m + jax.nn.sigmoid(gb_ref[...].astype(F32)) * _dot(yb_ref[...], wb_ref[...])
    m = m + jax.nn.sigmoid(gc_ref[...].astype(F32)) * _dot(yc_ref[...], wc_ref[...])
    o_ref[...] = m.astype(o_ref.dtype)


def _merge(ya, yb, yc, wa, wb, wc, z_main2, rows, d):
    bm = _pick(rows, 512)
    bn = _pick(d, 1024)
    g0 = MAIN_OFF_G // bn
    gstep = d // bn
    kin = ya.shape[1] + yb.shape[1] + yc.shape[1]
    blocks = _nbytes((bm, kin), BF16) + _nbytes((kin, bn), BF16) + 4 * _nbytes((bm, bn), BF16)
    gate = lambda k: pl.BlockSpec((bm, bn), lambda i, j: (i, g0 + k * gstep + j))
    return pl.pallas_call(
        _merge_kernel,
        grid=(rows // bm, d // bn),
        in_specs=[
            pl.BlockSpec((bm, ya.shape[1]), lambda i, j: (i, 0)),
            pl.BlockSpec((bm, yb.shape[1]), lambda i, j: (i, 0)),
            pl.BlockSpec((bm, yc.shape[1]), lambda i, j: (i, 0)),
            pl.BlockSpec((wa.shape[0], bn), lambda i, j: (0, j)),
            pl.BlockSpec((wb.shape[0], bn), lambda i, j: (0, j)),
            pl.BlockSpec((wc.shape[0], bn), lambda i, j: (0, j)),
            gate(0), gate(1), gate(2),
        ],
        out_specs=pl.BlockSpec((bm, bn), lambda i, j: (i, j)),
        out_shape=jax.ShapeDtypeStruct((rows, d), BF16),
        compiler_params=_params(("arbitrary", "arbitrary"), blocks, 5 * _nbytes((bm, bn), F32)),
        name="branch_merge",
    )(ya, yb, yc, wa, wb, wc, z_main2, z_main2, z_main2)


def _out_proj_kernel(m_ref, w_ref, h_ref, g_ref, o_ref):
    o_ref[...] = h_ref[...] + g_ref[0] * _dot(m_ref[...], w_ref[...])


def _out_proj(m, w, h2, mod3, gate_blk, rows, n):
    d = w.shape[1]
    bm = _pick(n, 512)
    bn = _pick(d, 1024)
    per_batch = n // bm
    gstep = d // bn
    blocks = _nbytes((bm, d), BF16) + _nbytes((d, bn), BF16) + 2 * _nbytes((bm, bn), F32)
    return pl.pallas_call(
        _out_proj_kernel,
        grid=(rows // bm, d // bn),
        in_specs=[
            pl.BlockSpec((bm, d), lambda i, j: (i, 0)),
            pl.BlockSpec((d, bn), lambda i, j: (0, j)),
            pl.BlockSpec((bm, bn), lambda i, j: (i, j)),
            pl.BlockSpec((1, 1, bn), lambda i, j: (i // per_batch, 0, gate_blk * gstep + j)),
        ],
        out_specs=pl.BlockSpec((bm, bn), lambda i, j: (i, j)),
        out_shape=jax.ShapeDtypeStruct((rows, d), F32),
        compiler_params=_params(("arbitrary", "arbitrary"), blocks, 2 * _nbytes((bm, bn), F32)),
        name="out_proj_residual",
    )(m, w, h2, mod3)


def _ffn_pre_kernel(x_ref, g_ref, sh_ref, sc_ref, wr_ref, br_ref, v_ref, idx_ref, wt_ref):
    d = x_ref.shape[2]
    v = _rms(x_ref[0], g_ref[...]) * (1.0 + sc_ref[0]) + sh_ref[0]
    v_ref[0] = _pack_bf16_pair(v[:, :d // 2], v[:, d // 2:])
    logits = _dot(v.astype(BF16), wr_ref[...]) + br_ref[...]
    lane = lax.broadcasted_iota(jnp.int32, logits.shape, 1)
    lane_f = lane.astype(F32)
    vals, idxs = [], []
    for _ in range(TOP_K):
        m = logits.max(axis=-1, keepdims=True)
        pick = jnp.where(logits == m, lane_f, float(LANES)).min(axis=-1, keepdims=True)
        vals.append(m)
        idxs.append(pick.astype(jnp.int32))
        logits = jnp.where(lane_f == pick, -jnp.inf, logits)
    exps = [jnp.exp(val - vals[0]) for val in vals]
    den = exps[0]
    for e in exps[1:]:
        den = den + e
    idx_out = jnp.zeros(lane.shape, jnp.int32)
    wt_out = jnp.zeros(lane.shape, F32)
    for k in range(TOP_K):
        idx_out = jnp.where(lane == k, idxs[k], idx_out)
        wt_out = jnp.where(lane == k, exps[k] / den, wt_out)
    idx_ref[0] = idx_out
    wt_ref[0] = wt_out


def _ffn_pre(h, nb, gain, mod3, sh_blk, sc_blk, wr, br):
    _, n, d = h.shape
    bn = _pick(n, 256)
    blocks = (_nbytes((bn, d), F32) + _nbytes((bn, d // 2), U32) + 3 * _nbytes((1, d), F32)
              + _nbytes((d, LANES), BF16) + 2 * _nbytes((bn, LANES), F32))
    return pl.pallas_call(
        _ffn_pre_kernel,
        grid=(nb, n // bn),
        in_specs=[
            pl.BlockSpec((1, bn, d), lambda b, i: (b, i, 0)),
            pl.BlockSpec((1, d), lambda b, i: (0, 0)),
            pl.BlockSpec((1, 1, d), lambda b, i: (b, 0, sh_blk)),
            pl.BlockSpec((1, 1, d), lambda b, i: (b, 0, sc_blk)),
            pl.BlockSpec((d, LANES), lambda b, i: (0, 0)),
            pl.BlockSpec((1, LANES), lambda b, i: (0, 0)),
        ],
        out_specs=[
            pl.BlockSpec((1, bn, d // 2), lambda b, i: (b, i, 0)),
            pl.BlockSpec((1, bn, LANES), lambda b, i: (b, i, 0)),
            pl.BlockSpec((1, bn, LANES), lambda b, i: (b, i, 0)),
        ],
        out_shape=[
            jax.ShapeDtypeStruct((nb, n, d // 2), U32),
            jax.ShapeDtypeStruct((nb, n, LANES), jnp.int32),
            jax.ShapeDtypeStruct((nb, n, LANES), F32),
        ],
        compiler_params=_params(("arbitrary", "arbitrary"), blocks, 3 * _nbytes((bn, d), F32)),
        name="ffn_norm_router",
    )(h, gain.reshape(1, d), mod3, mod3, wr, br)


def _row_copy(src_ref, o_ref, sem, src_row, dst_row):
    return pltpu.make_async_copy(src_ref.at[pl.ds(src_row, 1)], o_ref.at[pl.ds(dst_row, 1)], sem)


def _gather_kernel(idx_ref, src_ref, o_ref, sem):
    bm = o_ref.shape[0]

    def issue(r, carry):
        _row_copy(src_ref, o_ref, sem, idx_ref[0, 0, r], r).start()
        return carry

    lax.fori_loop(0, bm, issue, 0)

    def wait(r, carry):
        _row_copy(src_ref, o_ref, sem, 0, r).wait()
        return carry

    lax.fori_loop(0, bm, wait, 0)


def _gather_rows(src, idx, bm):
    rows = idx.shape[0]
    width = src.shape[1]
    nt = rows // bm
    return pl.pallas_call(
        _gather_kernel,
        grid=(nt,),
        in_specs=[
            pl.BlockSpec((1, 1, bm), lambda t: (t, 0, 0), memory_space=pltpu.SMEM),
            pl.BlockSpec(memory_space=pl.ANY),
        ],
        out_specs=pl.BlockSpec((bm, width), lambda t: (t, 0)),
        out_shape=jax.ShapeDtypeStruct((rows, width), src.dtype),
        scratch_shapes=[pltpu.SemaphoreType.DMA(())],
        compiler_params=_params(("arbitrary",), _nbytes((bm, width), src.dtype)),
        name="gather_rows",
    )(idx.reshape(nt, 1, bm), src)


def _expert_kernel(te_ref, tv_ref, x_ref, wgu_ref, bgu_ref, wd_ref, bd_ref, rw_ref, o_ref):
    t = pl.program_id(0)
    half = x_ref.shape[1]
    ff = wd_ref.shape[1]

    @pl.when(tv_ref[t] != 0)
    def _():
        lo, hi = _unpack_bf16_pair(x_ref[...])
        hgu = _dot(lo.astype(BF16), wgu_ref[0, :half]) + _dot(hi.astype(BF16), wgu_ref[0, half:]) + bgu_ref[0]
        gate = jnp.minimum(hgu[:, :ff], SWIGLU_LIMIT)
        lin = jnp.clip(hgu[:, ff:], -SWIGLU_LIMIT, SWIGLU_LIMIT)
        act = gate * jax.nn.sigmoid(SWIGLU_ALPHA * gate) * (lin + 1.0)
        y = (_dot(act.astype(BF16), wd_ref[0]) + bd_ref[0]) * rw_ref[...]
        o_ref[...] = _pack_bf16_pair(y[:, :half], y[:, half:])

    @pl.when(tv_ref[t] == 0)
    def _():
        o_ref[...] = jnp.zeros(o_ref.shape, o_ref.dtype)


def _experts(xg, tile_expert, tile_valid, wgu, bgu, wd, bd, row_w, bm):
    rows, half = xg.shape
    e, d, ff2 = wgu.shape
    ff = ff2 // 2
    nt = rows // bm
    blocks = (2 * _nbytes((bm, half), U32) + _nbytes((d, ff2), BF16) + _nbytes((ff, d), BF16)
              + _nbytes((1, ff2 + d), F32) + _nbytes((bm, LANES), F32))
    temps = 2 * _nbytes((bm, d), F32) + 2 * _nbytes((bm, d), BF16) + 3 * _nbytes((bm, ff2), F32)
    grid_spec = pltpu.PrefetchScalarGridSpec(
        num_scalar_prefetch=2,
        grid=(nt,),
        in_specs=[
            pl.BlockSpec((bm, half), lambda t, te, tv: (t, 0)),
            pl.BlockSpec((1, d, ff2), lambda t, te, tv: (te[t], 0, 0)),
            pl.BlockSpec((1, 1, ff2), lambda t, te, tv: (te[t], 0, 0)),
            pl.BlockSpec((1, ff, d), lambda t, te, tv: (te[t], 0, 0)),
            pl.BlockSpec((1, 1, d), lambda t, te, tv: (te[t], 0, 0)),
            pl.BlockSpec((bm, 1), lambda t, te, tv: (t, 0)),
        ],
        out_specs=pl.BlockSpec((bm, half), lambda t, te, tv: (t, 0)),
    )
    return pl.pallas_call(
        _expert_kernel,
        grid_spec=grid_spec,
        out_shape=jax.ShapeDtypeStruct((rows, half), U32),
        compiler_params=_params(("arbitrary",), blocks, temps),
        name="experts",
    )(tile_expert, tile_valid, xg, wgu, bgu.reshape(e, 1, ff2), wd, bd.reshape(e, 1, d), row_w.reshape(rows, 1))


def _combine_kernel(y_ref, h_ref, g_ref, gf_ref, o_ref, *, final):
    half = h_ref.shape[1] // 2
    f_lo = None
    f_hi = None
    for k in range(TOP_K):
        lo, hi = _unpack_bf16_pair(y_ref[:, k * half:(k + 1) * half])
        f_lo = lo if f_lo is None else f_lo + lo
        f_hi = hi if f_hi is None else f_hi + hi
    o_lo = h_ref[:, :half] + g_ref[0, :, :half] * f_lo
    o_hi = h_ref[:, half:] + g_ref[0, :, half:] * f_hi
    if final:
        ms = (jnp.sum(o_lo * o_lo, axis=-1, keepdims=True) + jnp.sum(o_hi * o_hi, axis=-1, keepdims=True)) / (2 * half)
        inv = lax.rsqrt(ms + NORM_EPS)
        o_lo = o_lo * inv * gf_ref[:, :half]
        o_hi = o_hi * inv * gf_ref[:, half:]
    o_ref[:, :half] = o_lo
    o_ref[:, half:] = o_hi


def _combine(yk, h2, mod3, gate_blk, gain_final, rows, n, final):
    d = h2.shape[1]
    bm = _pick(n, 256)
    per_batch = n // bm
    blocks = _nbytes((bm, TOP_K * d // 2), U32) + 2 * _nbytes((bm, d), F32) + 2 * _nbytes((1, d), F32)
    return pl.pallas_call(
        functools.partial(_combine_kernel, final=final),
        grid=(rows // bm,),
        in_specs=[
            pl.BlockSpec((bm, TOP_K * d // 2), lambda i: (i, 0)),
            pl.BlockSpec((bm, d), lambda i: (i, 0)),
            pl.BlockSpec((1, 1, d), lambda i: (i // per_batch, 0, gate_blk)),
            pl.BlockSpec((1, d), lambda i: (0, 0)),
        ],
        out_specs=pl.BlockSpec((bm, d), lambda i: (i, 0)),
        out_shape=jax.ShapeDtypeStruct((rows, d), F32),
        compiler_params=_params(("arbitrary",), blocks, 4 * _nbytes((bm, d), F32)),
        name="moe_combine",
    )(yk, h2, mod3, gain_final.reshape(1, d))


def _rope_tables(n, rot_dim):
    t = jnp.arange(n, dtype=jnp.int32)
    row = (t // GRID_W).astype(F32)
    col = (t % GRID_W).astype(F32)
    half = rot_dim // 2
    inv = 1.0 / (ROPE_THETA ** (jnp.arange(0, half, 2, dtype=F32) / half))
    ang_r = row[:, None] * inv[None, :]
    ang_c = col[:, None] * inv[None, :]
    cos = jnp.concatenate([jnp.cos(ang_r), jnp.cos(ang_r), jnp.cos(ang_c), jnp.cos(ang_c)], axis=1)
    sin = jnp.concatenate([-jnp.sin(ang_r), jnp.sin(ang_r), -jnp.sin(ang_c), jnp.sin(ang_c)], axis=1)
    pad = LANES - rot_dim
    cos = jnp.pad(cos, ((0, 0), (0, pad)), constant_values=1.0)
    sin = jnp.pad(sin, ((0, 0), (0, pad)))
    return jnp.stack([cos, jnp.ones_like(cos)]), jnp.stack([sin, jnp.zeros_like(sin)])


def _dft_cos_sin(n):
    k = jnp.arange(n, dtype=jnp.int32)
    ang = ((k[:, None] * k[None, :]) % n).astype(F32) * (2.0 * math.pi / n)
    return jnp.cos(ang), jnp.sin(ang)


def _fourier_matrices(n, n_sub):
    cc, sc = _dft_cos_sin(FOURIER_GROUP_W)
    w_chan = (jnp.concatenate([cc, sc], axis=1) * FOURIER_GROUP_W ** -0.5).astype(BF16)
    cn, sn = _dft_cos_sin(n)
    lat = jnp.concatenate([cn, -sn], axis=1) * n ** -0.5
    m = n // n_sub
    cm, sm = _dft_cos_sin(m)
    eye = jnp.eye(n_sub, dtype=F32)
    ctx = jnp.concatenate([jnp.kron(eye, cm), -jnp.kron(eye, sm)], axis=1) * m ** -0.5
    return w_chan, jnp.stack([lat, ctx]).astype(BF16)


def _layer_weights(l, w_in, mla_w_uq, mla_w_ukv, w_br_gqa, w_br_mla, w_br_fourier, w_out, w_router, b_router,
                   w_gate_up, b_gate_up, w_down):
    d = w_in.shape[1]
    wi = w_in[l]
    kv_cols = 2 * GQA_KV_W + MLA_KV_LORA + MLA_ROPE_DIM
    w_kv = jnp.pad(wi[:, :kv_cols], ((0, 0), (0, KV_W - kv_cols))).astype(BF16)
    w_main = wi[:, kv_cols:].astype(BF16)
    uq = mla_w_uq[l].reshape(MLA_Q_LORA, MLA_HEADS, MLA_QK_DIM)
    uq_rope = jnp.pad(uq[:, :, MLA_NOPE_DIM:], ((0, 0), (0, 0), (0, LANES - MLA_ROPE_DIM)))
    w_uq = jnp.concatenate([uq[:, :, :MLA_NOPE_DIM].reshape(MLA_Q_LORA, -1), uq_rope.reshape(MLA_Q_LORA, -1)],
                           axis=1).astype(BF16)
    ukv = mla_w_ukv[l].reshape(MLA_KV_LORA, MLA_HEADS, MLA_NOPE_DIM + MLA_V_DIM)
    w_ukv = jnp.concatenate([ukv[:, :, :MLA_NOPE_DIM].reshape(MLA_KV_LORA, -1),
                             ukv[:, :, MLA_NOPE_DIM:].reshape(MLA_KV_LORA, -1)], axis=1).astype(BF16)
    e = w_router.shape[2]
    wr = jnp.pad(w_router[l], ((0, 0), (0, LANES - e))).astype(BF16)
    br = jnp.pad(b_router[l], (0, LANES - e), constant_values=NEG_BIG).reshape(1, LANES)
    wgu = jnp.concatenate([w_gate_up[l][..., 0::2], w_gate_up[l][..., 1::2]], axis=-1).astype(BF16)
    bgu = jnp.concatenate([b_gate_up[l][..., 0::2], b_gate_up[l][..., 1::2]], axis=-1)
    return dict(w_kv=w_kv, w_main=w_main, w_uq=w_uq, w_ukv=w_ukv, w_br_gqa=w_br_gqa[l].astype(BF16),
                w_br_mla=w_br_mla[l].astype(BF16), w_br_fourier=w_br_fourier[l].astype(BF16),
                w_out=w_out[l].astype(BF16), wr=wr, br=br, wgu=wgu, bgu=bgu, wd=w_down[l].astype(BF16))


def _route(idx, wts, n_experts, bm):
    pairs = idx.shape[0] * TOP_K
    flat_e = idx.reshape(pairs)
    flat_w = wts.reshape(pairs)
    onehot = (flat_e[:, None] == jnp.arange(n_experts, dtype=jnp.int32)[None, :]).astype(jnp.int32)
    csum = jnp.cumsum(onehot, axis=0)
    counts = csum[-1]
    tiles_e = (counts + bm - 1) // bm
    tile_end = jnp.cumsum(tiles_e)
    tile_start = tile_end - tiles_e
    dest = jnp.sum(onehot * (csum - 1 + tile_start[None, :] * bm), axis=1)
    n_tiles = pairs // bm + n_experts
    rows = n_tiles * bm
    row_token = jnp.zeros((rows,), jnp.int32).at[dest].set(jnp.arange(pairs, dtype=jnp.int32) // TOP_K)
    row_w = jnp.zeros((rows,), F32).at[dest].set(flat_w)
    tile_ids = jnp.arange(n_tiles, dtype=jnp.int32)
    tile_expert = jnp.minimum(jnp.sum((tile_ids[:, None] >= tile_end[None, :]).astype(jnp.int32), axis=1),
                              n_experts - 1)
    tile_valid = (tile_ids < tile_end[-1]).astype(jnp.int32)
    return row_token, row_w, dest, tile_expert, tile_valid


def _layer(h_all, mod3, lw, gains, tabs, fmats, n_lat, ctx_len, last, norm_final):
    nb_all, n, d = h_all.shape
    nb = n_lat if last else nb_all
    rows = nb * n
    cg, sg, cm, sm = tabs
    w_chan, w_pos = fmats

    u = _norm_mod(h_all, gains["norm_mix"], mod3, 0, 1)
    u2 = u.reshape(nb_all * n, d)
    z_kv = _matmul(u2, lw["w_kv"], nb_all * n).reshape(nb_all, n, KV_W)
    z_main2 = _matmul(u2, lw["w_main"], rows)
    z_main = z_main2.reshape(nb, n, -1)
    ka, kvb, kr = _prep_kv(z_kv, gains["gqa_k_norm"], gains["mla_kv_norm"], lw["w_ukv"], (cg, sg, cm, sm), n_lat)
    qa, qb = _prep_q(z_main, nb, gains["gqa_q_norm"], gains["mla_q_norm"], lw["w_uq"], (cg, sg, cm, sm), n_lat)
    y_a = _gqa_attention(qa, ka, z_kv, nb, n_lat, ctx_len)
    y_b = _mla_attention(qb, kvb, kr, nb, n_lat, ctx_len)
    y_c = _fourier(z_main, nb, n_lat, w_chan, w_pos)
    m = _merge(y_a.reshape(rows, -1), y_b.reshape(rows, -1), y_c.reshape(rows, -1),
               lw["w_br_gqa"], lw["w_br_mla"], lw["w_br_fourier"], z_main2, rows, d)
    h2 = _out_proj(m, lw["w_out"], h_all.reshape(nb_all * n, d), mod3, 2, rows, n)

    v_packed, idx, wts = _ffn_pre(h2.reshape(nb, n, d), nb, gains["norm_ffn"], mod3, 3, 4, lw["wr"], lw["br"])
    n_experts = lw["wgu"].shape[0]
    bm = 256
    row_token, row_w, dest, tile_expert, tile_valid = _route(
        idx.reshape(rows, LANES)[:, :TOP_K], wts.reshape(rows, LANES)[:, :TOP_K], n_experts, bm)
    xg = _gather_rows(v_packed.reshape(rows, d // 2), row_token, bm)
    y_sorted = _experts(xg, tile_expert, tile_valid, lw["wgu"], lw["bgu"], lw["wd"], lw["bd"], row_w, bm)
    yk = _gather_rows(y_sorted, dest, bm).reshape(rows, TOP_K * d // 2)
    out = _combine(yk, h2, mod3, 5, norm_final, rows, n, last)
    return out.reshape(nb, n, d)


def kernel(x, c, ctx, c_ctx, w_ada, b_ada, norm_mix, w_in, gqa_q_norm, gqa_k_norm, mla_q_norm, mla_w_uq, mla_kv_norm, mla_w_ukv, w_br_gqa, w_br_mla, w_br_fourier, w_out, norm_ffn, w_router, b_router, w_gate_up, b_gate_up, w_down, b_down, norm_final):
    n_lat, n, d = x.shape
    ctx_len = ctx.shape[1]
    depth = w_ada.shape[0]
    assert n_lat * ctx_len == n, "context tokens must fill exactly one extra batch row"
    assert n % GRID_W == 0 and ctx_len % 8 == 0

    tabs_g = _rope_tables(n, HEAD_DIM)
    tabs_m = _rope_tables(n, MLA_ROPE_DIM)
    tabs = (tabs_g[0], tabs_g[1], tabs_m[0], tabs_m[1])
    fmats = _fourier_matrices(n, n_lat)

    mod_rows = 16
    c_all = jnp.zeros((mod_rows, d), F32).at[:n_lat].set(c).at[n_lat].set(c_ctx)
    mod = _ada_mod(c_all, w_ada, b_ada)

    h_all = jnp.concatenate([x, ctx.reshape(1, n, d)], axis=0)
    for l in range(depth):
        last = l == depth - 1
        lw = _layer_weights(l, w_in, mla_w_uq, mla_w_ukv, w_br_gqa, w_br_mla, w_br_fourier, w_out, w_router,
                            b_router, w_gate_up, b_gate_up, w_down)
        lw["bd"] = b_down[l]
        gains = dict(norm_mix=norm_mix[l], gqa_q_norm=gqa_q_norm[l], gqa_k_norm=gqa_k_norm[l],
                     mla_q_norm=mla_q_norm[l], mla_kv_norm=mla_kv_norm[l], norm_ffn=norm_ffn[l])
        mod3 = mod[l].reshape(mod_rows, 1, 6 * d)
        h_all = _layer(h_all, mod3, lw, gains, tabs, fmats, n_lat, ctx_len, last, norm_final)
    return h_all
```

```python
import functools
import math

import jax
import jax.numpy as jnp
from jax import lax
from jax.experimental import pallas as pl
from jax.experimental.pallas import tpu as pltpu

F32 = jnp.float32
BF16 = jnp.bfloat16
U32 = jnp.uint32

GRID_W = 64
ROPE_THETA = 10000.0
NORM_EPS = 1e-6
HEAD_DIM = 128
GQA_HEADS = 16
GQA_KV_HEADS = 4
GQA_GROUP = GQA_HEADS // GQA_KV_HEADS
GQA_Q_W = GQA_HEADS * HEAD_DIM
GQA_KV_W = GQA_KV_HEADS * HEAD_DIM
GQA_SCALE = HEAD_DIM ** -0.5
MLA_HEADS = 8
MLA_Q_LORA = 1024
MLA_KV_LORA = 512
MLA_NOPE_DIM = 128
MLA_ROPE_DIM = 64
MLA_V_DIM = 128
MLA_QK_DIM = MLA_NOPE_DIM + MLA_ROPE_DIM
MLA_OUT_W = MLA_HEADS * MLA_V_DIM
MLA_SCALE = MLA_QK_DIM ** -0.5
FOURIER_GROUPS = 4
FOURIER_GROUP_W = 256
FOURIER_W = FOURIER_GROUPS * FOURIER_GROUP_W
TOP_K = 4
SWIGLU_ALPHA = 1.702
SWIGLU_LIMIT = 7.0

LANES = 128
V7X_VMEM_BYTES = 64 * 2 ** 20
VMEM_CAP = V7X_VMEM_BYTES - 6 * 2 ** 20
VMEM_FLOOR = 32 * 2 ** 20
VMEM_INTERNAL = 8 * 2 ** 20

KV_W = 2048
KV_OFF_K, KV_OFF_V, KV_OFF_CKV, KV_OFF_KR = 0, 512, 1024, 1536
MAIN_OFF_QA, MAIN_OFF_CQ, MAIN_OFF_F, MAIN_OFF_G = 0, 2048, 3072, 4096
NEG_BIG = -1e30


def _vmem(block_bytes, temp_bytes=0):
    return int(min(VMEM_CAP, max(VMEM_FLOOR, 2 * block_bytes + temp_bytes + VMEM_INTERNAL)))


def _params(sem, block_bytes, temp_bytes=0):
    return pltpu.CompilerParams(dimension_semantics=sem, vmem_limit_bytes=_vmem(block_bytes, temp_bytes))


def _nbytes(shape, dtype):
    return math.prod(shape) * jnp.dtype(dtype).itemsize


def _pick(n, target):
    if n <= target:
        return n
    for cand in range(target, 7, -1):
        if n % cand == 0 and cand % 8 == 0:
            return cand
    return n


def _rms(x, gain):
    return x * lax.rsqrt(jnp.mean(x * x, axis=-1, keepdims=True) + NORM_EPS) * gain


def _rope(x, cos, sin, blk):
    lane = lax.broadcasted_iota(jnp.int32, x.shape, x.ndim - 1)
    first = (lane % (2 * blk)) < blk
    swapped = jnp.where(first, pltpu.roll(x, LANES - blk, x.ndim - 1), pltpu.roll(x, blk, x.ndim - 1))
    return x * cos + swapped * sin


def _pack_bf16_pair(lo, hi):
    lo_bits = lax.bitcast_convert_type(lo.astype(BF16).astype(F32), U32) >> 16
    hi_bits = lax.bitcast_convert_type(hi.astype(BF16).astype(F32), U32) & jnp.uint32(0xFFFF0000)
    return lo_bits | hi_bits


def _unpack_bf16_pair(word):
    lo = lax.bitcast_convert_type(word << 16, F32)
    hi = lax.bitcast_convert_type(word & jnp.uint32(0xFFFF0000), F32)
    return lo, hi


def _dot(a, b):
    return jnp.dot(a, b, preferred_element_type=F32)


def _dot_nt(a, b):
    return lax.dot_general(a, b, (((1,), (1,)), ((), ())), preferred_element_type=F32)


def _ada_kernel(c_ref, w_ref, b_ref, o_ref):
    c = c_ref[...]
    act = (c * jax.nn.sigmoid(c)).astype(BF16)
    o_ref[0] = _dot(act, w_ref[0].astype(BF16)) + b_ref[0]


def _ada_mod(c_all, w_ada, b_ada):
    depth, d, n6 = w_ada.shape
    rows = c_all.shape[0]
    bn = _pick(n6, 512)
    blocks = _nbytes((rows, d), F32) + _nbytes((d, bn), F32) + _nbytes((rows, bn), F32) * 2
    return pl.pallas_call(
        _ada_kernel,
        grid=(depth, n6 // bn),
        in_specs=[
            pl.BlockSpec((rows, d), lambda l, j: (0, 0)),
            pl.BlockSpec((1, d, bn), lambda l, j: (l, 0, j)),
            pl.BlockSpec((1, 1, bn), lambda l, j: (l, 0, j)),
        ],
        out_specs=pl.BlockSpec((1, rows, bn), lambda l, j: (l, 0, j)),
        out_shape=jax.ShapeDtypeStruct((depth, rows, n6), F32),
        compiler_params=_params(("arbitrary", "arbitrary"), blocks, _nbytes((d, bn), BF16)),
        name="adaln_mod",
    )(c_all, w_ada, b_ada.reshape(depth, 1, n6))


def _norm_mod_kernel(x_ref, g_ref, sh_ref, sc_ref, o_ref):
    y = _rms(x_ref[0], g_ref[...])
    o_ref[0] = (y * (1.0 + sc_ref[0]) + sh_ref[0]).astype(o_ref.dtype)


def _norm_mod(h, gain, mod3, sh_blk, sc_blk):
    nb, n, d = h.shape
    bn = _pick(n, 256)
    blocks = _nbytes((bn, d), F32) + _nbytes((bn, d), BF16) + 3 * _nbytes((1, d), F32)
    return pl.pallas_call(
        _norm_mod_kernel,
        grid=(nb, n // bn),
        in_specs=[
            pl.BlockSpec((1, bn, d), lambda b, i: (b, i, 0)),
            pl.BlockSpec((1, d), lambda b, i: (0, 0)),
            pl.BlockSpec((1, 1, d), lambda b, i: (b, 0, sh_blk)),
            pl.BlockSpec((1, 1, d), lambda b, i: (b, 0, sc_blk)),
        ],
        out_specs=pl.BlockSpec((1, bn, d), lambda b, i: (b, i, 0)),
        out_shape=jax.ShapeDtypeStruct((nb, n, d), BF16),
        compiler_params=_params(("arbitrary", "arbitrary"), blocks, 2 * _nbytes((bn, d), F32)),
        name="norm_mod",
    )(h, gain.reshape(1, d), mod3, mod3)


def _mm_kernel(x_ref, w_ref, o_ref):
    o_ref[...] = _dot(x_ref[...], w_ref[...]).astype(o_ref.dtype)


def _matmul(x2, w, rows, out_dtype=BF16):
    k = x2.shape[1]
    n = w.shape[1]
    bm = _pick(rows, 1024)
    bn = _pick(n, 1024)
    blocks = _nbytes((bm, k), BF16) + _nbytes((k, bn), BF16) + _nbytes((bm, bn), out_dtype)
    return pl.pallas_call(
        _mm_kernel,
        grid=(rows // bm, n // bn),
        in_specs=[
            pl.BlockSpec((bm, k), lambda i, j: (i, 0)),
            pl.BlockSpec((k, bn), lambda i, j: (0, j)),
        ],
        out_specs=pl.BlockSpec((bm, bn), lambda i, j: (i, j)),
        out_shape=jax.ShapeDtypeStruct((rows, n), out_dtype),
        compiler_params=_params(("arbitrary", "arbitrary"), blocks, _nbytes((bm, bn), F32)),
        name="matmul",
    )(x2, w)


def _prep_kv_kernel(z_ref, gk_ref, gkv_ref, wukv_ref, cg_ref, sg_ref, cm_ref, sm_ref, ka_ref, kvb_ref, kr_ref):
    gk = gk_ref[...]
    cg, sg = cg_ref[0], sg_ref[0]
    for h in range(GQA_KV_HEADS):
        sl = slice(KV_OFF_K + h * HEAD_DIM, KV_OFF_K + (h + 1) * HEAD_DIM)
        y = _rope(_rms(z_ref[0, :, sl].astype(F32), gk), cg, sg, HEAD_DIM // 4)
        ka_ref[0, :, h * HEAD_DIM:(h + 1) * HEAD_DIM] = y.astype(ka_ref.dtype)
    ckv = _rms(z_ref[0, :, KV_OFF_CKV:KV_OFF_CKV + MLA_KV_LORA].astype(F32), gkv_ref[...])
    kvb_ref[0] = _dot(ckv.astype(BF16), wukv_ref[...]).astype(kvb_ref.dtype)
    kr = z_ref[0, :, KV_OFF_KR:KV_OFF_KR + LANES].astype(F32)
    kr_ref[0] = _rope(kr, cm_ref[0], sm_ref[0], MLA_ROPE_DIM // 4).astype(kr_ref.dtype)


def _prep_kv(z_kv, gk, gkv, wukv, tabs, n_lat):
    nb, n, _ = z_kv.shape
    bn = _pick(n, 512)
    kvw = wukv.shape[1]
    tab = pl.BlockSpec((1, bn, LANES), lambda b, i: (b // n_lat, i, 0))
    blocks = (_nbytes((bn, KV_W), BF16) + _nbytes(wukv.shape, BF16) + 4 * _nbytes((bn, LANES), F32)
              + _nbytes((bn, GQA_KV_W + kvw + LANES), BF16))
    return pl.pallas_call(
        _prep_kv_kernel,
        grid=(nb, n // bn),
        in_specs=[
            pl.BlockSpec((1, bn, KV_W), lambda b, i: (b, i, 0)),
            pl.BlockSpec((1, HEAD_DIM), lambda b, i: (0, 0)),
            pl.BlockSpec((1, MLA_KV_LORA), lambda b, i: (0, 0)),
            pl.BlockSpec(wukv.shape, lambda b, i: (0, 0)),
            tab, tab, tab, tab,
        ],
        out_specs=[
            pl.BlockSpec((1, bn, GQA_KV_W), lambda b, i: (b, i, 0)),
            pl.BlockSpec((1, bn, kvw), lambda b, i: (b, i, 0)),
            pl.BlockSpec((1, bn, LANES), lambda b, i: (b, i, 0)),
        ],
        out_shape=[
            jax.ShapeDtypeStruct((nb, n, GQA_KV_W), BF16),
            jax.ShapeDtypeStruct((nb, n, kvw), BF16),
            jax.ShapeDtypeStruct((nb, n, LANES), BF16),
        ],
        compiler_params=_params(("arbitrary", "arbitrary"), blocks, 4 * _nbytes((bn, kvw), F32)),
        name="prep_kv",
    )(z_kv, gk.reshape(1, HEAD_DIM), gkv.reshape(1, MLA_KV_LORA), wukv, *tabs)


def _prep_q_kernel(zq_ref, zcq_ref, gq_ref, gcq_ref, wuq_ref, cg_ref, sg_ref, cm_ref, sm_ref, qa_ref, qb_ref):
    gq = gq_ref[...]
    cg, sg = cg_ref[0], sg_ref[0]
    for h in range(GQA_HEADS):
        sl = slice(h * HEAD_DIM, (h + 1) * HEAD_DIM)
        y = _rope(_rms(zq_ref[0, :, sl].astype(F32), gq), cg, sg, HEAD_DIM // 4)
        qa_ref[0, :, sl] = (y * GQA_SCALE).astype(qa_ref.dtype)
    cq = _rms(zcq_ref[0].astype(F32), gcq_ref[...])
    q = _dot(cq.astype(BF16), wuq_ref[...])
    nope_w = MLA_HEADS * MLA_NOPE_DIM
    qb_ref[0, :, :nope_w] = (q[:, :nope_w] * MLA_SCALE).astype(qb_ref.dtype)
    cm, sm = cm_ref[0], sm_ref[0]
    for h in range(MLA_HEADS):
        sl = slice(nope_w + h * LANES, nope_w + (h + 1) * LANES)
        qb_ref[0, :, sl] = (_rope(q[:, sl], cm, sm, MLA_ROPE_DIM // 4) * MLA_SCALE).astype(qb_ref.dtype)


def _prep_q(z_main, nb, gq, gcq, wuq, tabs, n_lat):
    n = z_main.shape[1]
    bn = _pick(n, 256)
    qbw = wuq.shape[1]
    tab = pl.BlockSpec((1, bn, LANES), lambda b, i: (b // n_lat, i, 0))
    blocks = (_nbytes((bn, GQA_Q_W + MLA_Q_LORA), BF16) + _nbytes(wuq.shape, BF16)
              + 4 * _nbytes((bn, LANES), F32) + _nbytes((bn, GQA_Q_W + qbw), BF16))
    return pl.pallas_call(
        _prep_q_kernel,
        grid=(nb, n // bn),
        in_specs=[
            pl.BlockSpec((1, bn, GQA_Q_W), lambda b, i: (b, i, MAIN_OFF_QA // GQA_Q_W)),
            pl.BlockSpec((1, bn, MLA_Q_LORA), lambda b, i: (b, i, MAIN_OFF_CQ // MLA_Q_LORA)),
            pl.BlockSpec((1, HEAD_DIM), lambda b, i: (0, 0)),
            pl.BlockSpec((1, MLA_Q_LORA), lambda b, i: (0, 0)),
            pl.BlockSpec(wuq.shape, lambda b, i: (0, 0)),
            tab, tab, tab, tab,
        ],
        out_specs=[
            pl.BlockSpec((1, bn, GQA_Q_W), lambda b, i: (b, i, 0)),
            pl.BlockSpec((1, bn, qbw), lambda b, i: (b, i, 0)),
        ],
        out_shape=[
            jax.ShapeDtypeStruct((nb, n, GQA_Q_W), BF16),
            jax.ShapeDtypeStruct((nb, n, qbw), BF16),
        ],
        compiler_params=_params(("arbitrary", "arbitrary"), blocks, 4 * _nbytes((bn, qbw), F32)),
        name="prep_q",
    )(z_main, z_main, gq.reshape(1, HEAD_DIM), gcq.reshape(1, MLA_Q_LORA), wuq, *tabs)


def _softmax_pv(scores, values):
    m = scores[0].max(axis=-1, keepdims=True)
    for s in scores[1:]:
        m = jnp.maximum(m, s.max(axis=-1, keepdims=True))
    acc = None
    den = None
    for s, v in zip(scores, values):
        p = jnp.exp(s - m)
        l = p.sum(axis=-1, keepdims=True)
        o = _dot(p.astype(v.dtype), v)
        acc = o if acc is None else acc + o
        den = l if den is None else den + l
    return acc / den


def _gqa_kernel(q_ref, kc_ref, kl_ref, vc_ref, vl_ref, o_ref, *, n_lat, has_ctx_batch):
    bq = q_ref.shape[1]
    q = q_ref[0]
    qs = jnp.concatenate([q[:, g * HEAD_DIM:(g + 1) * HEAD_DIM] for g in range(GQA_GROUP)], axis=0)

    def finish(scores, values):
        o = _softmax_pv(scores, values)
        for g in range(GQA_GROUP):
            o_ref[0, :, g * HEAD_DIM:(g + 1) * HEAD_DIM] = o[g * bq:(g + 1) * bq].astype(o_ref.dtype)

    def latent():
        finish([_dot_nt(qs, kc_ref[0]), _dot_nt(qs, kl_ref[0])], [vc_ref[0], vl_ref[0]])

    def context():
        finish([_dot_nt(qs, kc_ref[0])], [vc_ref[0]])

    if has_ctx_batch:
        b = pl.program_id(0)
        pl.when(b < n_lat)(latent)
        pl.when(b == n_lat)(context)
    else:
        latent()


def _mla_kernel(qn_ref, qr_ref, knc_ref, krc_ref, knl_ref, krl_ref, vc_ref, vl_ref, o_ref, *, n_lat, has_ctx_batch):
    q = jnp.concatenate([qn_ref[0], qr_ref[0]], axis=1)

    def finish(scores, values):
        o_ref[0] = _softmax_pv(scores, values).astype(o_ref.dtype)

    def latent():
        kc = jnp.concatenate([knc_ref[0], krc_ref[0]], axis=1)
        kl = jnp.concatenate([knl_ref[0], krl_ref[0]], axis=1)
        finish([_dot_nt(q, kc), _dot_nt(q, kl)], [vc_ref[0], vl_ref[0]])

    def context():
        kc = jnp.concatenate([knc_ref[0], krc_ref[0]], axis=1)
        finish([_dot_nt(q, kc)], [vc_ref[0]])

    if has_ctx_batch:
        b = pl.program_id(0)
        pl.when(b < n_lat)(latent)
        pl.when(b == n_lat)(context)
    else:
        latent()


def _ctx_idx(b, i, n_lat):
    return jnp.where(b < n_lat, b, i)


def _lat_idx(b, n_lat):
    return jnp.where(b < n_lat, b, 0)


def _gqa_attention(qa, ka, z_kv, nb, n_lat, ctx_len):
    n = qa.shape[1]
    bq = ctx_len
    qw = GQA_GROUP * HEAD_DIM
    v_blk = KV_OFF_V // HEAD_DIM
    kern = functools.partial(_gqa_kernel, n_lat=n_lat, has_ctx_batch=nb > n_lat)
    blocks = (2 * _nbytes((bq, qw), BF16) + 2 * _nbytes((ctx_len + n, HEAD_DIM), BF16))
    temps = 3 * _nbytes((GQA_GROUP * bq, ctx_len + n), F32)
    return pl.pallas_call(
        kern,
        grid=(nb, GQA_KV_HEADS, n // bq),
        in_specs=[
            pl.BlockSpec((1, bq, qw), lambda b, h, i: (b, i, h)),
            pl.BlockSpec((1, ctx_len, HEAD_DIM), lambda b, h, i: (n_lat, _ctx_idx(b, i, n_lat), h)),
            pl.BlockSpec((1, n, HEAD_DIM), lambda b, h, i: (_lat_idx(b, n_lat), 0, h)),
            pl.BlockSpec((1, ctx_len, HEAD_DIM), lambda b, h, i: (n_lat, _ctx_idx(b, i, n_lat), v_blk + h)),
            pl.BlockSpec((1, n, HEAD_DIM), lambda b, h, i: (_lat_idx(b, n_lat), 0, v_blk + h)),
        ],
        out_specs=pl.BlockSpec((1, bq, qw), lambda b, h, i: (b, i, h)),
        out_shape=jax.ShapeDtypeStruct((nb, n, GQA_Q_W), BF16),
        compiler_params=_params(("arbitrary", "arbitrary", "arbitrary"), blocks, temps),
        name="gqa_attention",
    )(qa, ka, ka, z_kv, z_kv)


def _mla_attention(qb, kvb, kr, nb, n_lat, ctx_len):
    n = qb.shape[1]
    bq = ctx_len
    kern = functools.partial(_mla_kernel, n_lat=n_lat, has_ctx_batch=nb > n_lat)
    blocks = 3 * _nbytes((bq, LANES), BF16) + 3 * _nbytes((ctx_len + n, LANES), BF16)
    temps = 3 * _nbytes((bq, ctx_len + n), F32) + 2 * _nbytes((ctx_len + n, 2 * LANES), BF16)
    ctx_spec = lambda col: pl.BlockSpec((1, ctx_len, LANES), lambda b, h, i: (n_lat, _ctx_idx(b, i, n_lat), col(h)))
    lat_spec = lambda col: pl.BlockSpec((1, n, LANES), lambda b, h, i: (_lat_idx(b, n_lat), 0, col(h)))
    return pl.pallas_call(
        kern,
        grid=(nb, MLA_HEADS, n // bq),
        in_specs=[
            pl.BlockSpec((1, bq, LANES), lambda b, h, i: (b, i, h)),
            pl.BlockSpec((1, bq, LANES), lambda b, h, i: (b, i, MLA_HEADS + h)),
            ctx_spec(lambda h: h), ctx_spec(lambda h: 0),
            lat_spec(lambda h: h), lat_spec(lambda h: 0),
            ctx_spec(lambda h: MLA_HEADS + h), lat_spec(lambda h: MLA_HEADS + h),
        ],
        out_specs=pl.BlockSpec((1, bq, LANES), lambda b, h, i: (b, i, h)),
        out_shape=jax.ShapeDtypeStruct((nb, n, MLA_OUT_W), BF16),
        compiler_params=_params(("arbitrary", "arbitrary", "arbitrary"), blocks, temps),
        name="mla_attention",
    )(qb, qb, kvb, kr, kvb, kr, kvb, kvb)


def _fourier_kernel(x_ref, wc_ref, wp_ref, o_ref):
    gw = x_ref.shape[2]
    xcs = _dot(x_ref[0], wc_ref[...]).astype(BF16)
    stacked = jnp.concatenate([xcs[:, :gw], xcs[:, gw:]], axis=0)
    o_ref[0] = _dot(wp_ref[0], stacked).astype(o_ref.dtype)


def _fourier(z_main, nb, n_lat, w_chan, w_pos):
    n = z_main.shape[1]
    gw = FOURIER_GROUP_W
    blocks = 2 * _nbytes((n, gw), BF16) + _nbytes((gw, 2 * gw), BF16) + _nbytes((n, 2 * n), BF16)
    temps = _nbytes((n, 2 * gw), F32) + 2 * _nbytes((2 * n, gw), BF16) + _nbytes((n, gw), F32)
    return pl.pallas_call(
        _fourier_kernel,
        grid=(nb, FOURIER_GROUPS),
        in_specs=[
            pl.BlockSpec((1, n, gw), lambda b, g: (b, 0, MAIN_OFF_F // gw + g)),
            pl.BlockSpec((gw, 2 * gw), lambda b, g: (0, 0)),
            pl.BlockSpec((1, n, 2 * n), lambda b, g: (b // n_lat, 0, 0)),
        ],
        out_specs=pl.BlockSpec((1, n, gw), lambda b, g: (b, 0, g)),
        out_shape=jax.ShapeDtypeStruct((nb, n, FOURIER_W), BF16),
        compiler_params=_params(("arbitrary", "arbitrary"), blocks, temps),
        name="fourier_mix",
    )(z_main, w_chan, w_pos)


def _merge_kernel(ya_ref, yb_ref, yc_ref, wa_ref, wb_ref, wc_ref, ga_ref, gb_ref, gc_ref, o_ref):
    m = jax.nn.sigmoid(ga_ref[...].astype(F32)) * _dot(ya_ref[...], wa_ref[...])
    m = m + jax.nn.sigmoid(gb_ref[...].astype(F32)) * _dot(yb_ref[...], wb_ref[...])
    m = m + jax.nn.sigmoid(gc_ref[...].astype(F32)) * _dot(yc_ref[...], wc_ref[...])
    o_ref[...] = m.astype(o_ref.dtype)


def _merge(ya, yb, yc, wa, wb, wc, z_main2, rows, d):
    bm = _pick(rows, 512)
    bn = _pick(d, 1024)
    g0 = MAIN_OFF_G // bn
    gstep = d // bn
    kin = ya.shape[1] + yb.shape[1] + yc.shape[1]
    blocks = _nbytes((bm, kin), BF16) + _nbytes((kin, bn), BF16) + 4 * _nbytes((bm, bn), BF16)
    gate = lambda k: pl.BlockSpec((bm, bn), lambda i, j: (i, g0 + k * gstep + j))
    return pl.pallas_call(
        _merge_kernel,
        grid=(rows // bm, d // bn),
        in_specs=[
            pl.BlockSpec((bm, ya.shape[1]), lambda i, j: (i, 0)),
            pl.BlockSpec((bm, yb.shape[1]), lambda i, j: (i, 0)),
            pl.BlockSpec((bm, yc.shape[1]), lambda i, j: (i, 0)),
            pl.BlockSpec((wa.shape[0], bn), lambda i, j: (0, j)),
            pl.BlockSpec((wb.shape[0], bn), lambda i, j: (0, j)),
            pl.BlockSpec((wc.shape[0], bn), lambda i, j: (0, j)),
            gate(0), gate(1), gate(2),
        ],
        out_specs=pl.BlockSpec((bm, bn), lambda i, j: (i, j)),
        out_shape=jax.ShapeDtypeStruct((rows, d), BF16),
        compiler_params=_params(("arbitrary", "arbitrary"), blocks, 5 * _nbytes((bm, bn), F32)),
        name="branch_merge",
    )(ya, yb, yc, wa, wb, wc, z_main2, z_main2, z_main2)


def _out_proj_kernel(m_ref, w_ref, h_ref, g_ref, o_ref):
    o_ref[...] = h_ref[...] + g_ref[0] * _dot(m_ref[...], w_ref[...])


def _out_proj(m, w, h2, mod3, gate_blk, rows, n):
    d = w.shape[1]
    bm = _pick(n, 512)
    bn = _pick(d, 1024)
    per_batch = n // bm
    gstep = d // bn
    blocks = _nbytes((bm, d), BF16) + _nbytes((d, bn), BF16) + 2 * _nbytes((bm, bn), F32)
    return pl.pallas_call(
        _out_proj_kernel,
        grid=(rows // bm, d // bn),
        in_specs=[
            pl.BlockSpec((bm, d), lambda i, j: (i, 0)),
            pl.BlockSpec((d, bn), lambda i, j: (0, j)),
            pl.BlockSpec((bm, bn), lambda i, j: (i, j)),
            pl.BlockSpec((1, 1, bn), lambda i, j: (i // per_batch, 0, gate_blk * gstep + j)),
        ],
        out_specs=pl.BlockSpec((bm, bn), lambda i, j: (i, j)),
        out_shape=jax.ShapeDtypeStruct((rows, d), F32),
        compiler_params=_params(("arbitrary", "arbitrary"), blocks, 2 * _nbytes((bm, bn), F32)),
        name="out_proj_residual",
    )(m, w, h2, mod3)


def _ffn_pre_kernel(x_ref, g_ref, sh_ref, sc_ref, wr_ref, br_ref, v_ref, idx_ref, wt_ref):
    d = x_ref.shape[2]
    v = _rms(x_ref[0], g_ref[...]) * (1.0 + sc_ref[0]) + sh_ref[0]
    v_ref[...] = _pack_bf16_pair(v[:, :d // 2], v[:, d // 2:]).reshape(v_ref.shape)
    logits = _dot(v.astype(BF16), wr_ref[...]) + br_ref[...]
    lane = lax.broadcasted_iota(jnp.int32, logits.shape, 1)
    lane_f = lane.astype(F32)
    vals, idxs = [], []
    for _ in range(TOP_K):
        m = logits.max(axis=-1, keepdims=True)
        pick = jnp.where(logits == m, lane_f, float(LANES)).min(axis=-1, keepdims=True)
        vals.append(m)
        idxs.append(pick.astype(jnp.int32))
        logits = jnp.where(lane_f == pick, -jnp.inf, logits)
    exps = [jnp.exp(val - vals[0]) for val in vals]
    den = exps[0]
    for e in exps[1:]:
        den = den + e
    idx_out = jnp.zeros(lane.shape, jnp.int32)
    wt_out = jnp.zeros(lane.shape, F32)
    for k in range(TOP_K):
        idx_out = jnp.where(lane == k, idxs[k], idx_out)
        wt_out = jnp.where(lane == k, exps[k] / den, wt_out)
    idx_ref[0] = idx_out
    wt_ref[0] = wt_out


def _ffn_pre(h, nb, gain, mod3, sh_blk, sc_blk, wr, br):
    _, n, d = h.shape
    bn = _pick(n, 256)
    blocks = (_nbytes((bn, d), F32) + _nbytes((bn, d // 2), U32) + 3 * _nbytes((1, d), F32)
              + _nbytes((d, LANES), BF16) + 2 * _nbytes((bn, LANES), F32))
    return pl.pallas_call(
        _ffn_pre_kernel,
        grid=(nb, n // bn),
        in_specs=[
            pl.BlockSpec((1, bn, d), lambda b, i: (b, i, 0)),
            pl.BlockSpec((1, d), lambda b, i: (0, 0)),
            pl.BlockSpec((1, 1, d), lambda b, i: (b, 0, sh_blk)),
            pl.BlockSpec((1, 1, d), lambda b, i: (b, 0, sc_blk)),
            pl.BlockSpec((d, LANES), lambda b, i: (0, 0)),
            pl.BlockSpec((1, LANES), lambda b, i: (0, 0)),
        ],
        out_specs=[
            pl.BlockSpec((bn, d // 2 // LANES, LANES), lambda b, i: (b * (n // bn) + i, 0, 0)),
            pl.BlockSpec((1, bn, LANES), lambda b, i: (b, i, 0)),
            pl.BlockSpec((1, bn, LANES), lambda b, i: (b, i, 0)),
        ],
        out_shape=[
            jax.ShapeDtypeStruct((nb * n, d // 2 // LANES, LANES), U32),
            jax.ShapeDtypeStruct((nb, n, LANES), jnp.int32),
            jax.ShapeDtypeStruct((nb, n, LANES), F32),
        ],
        compiler_params=_params(("arbitrary", "arbitrary"), blocks, 3 * _nbytes((bn, d), F32)),
        name="ffn_norm_router",
    )(h, gain.reshape(1, d), mod3, mod3, wr, br)


GATHER_UNROLL = 8


def _start_row_gather(src_hbm, idx_ref, buf, sem, slot):
    def body(r, carry):
        pltpu.make_async_copy(src_hbm.at[idx_ref[0, 0, r]], buf.at[slot, r], sem.at[slot]).start()
        return carry

    lax.fori_loop(0, buf.shape[1], body, 0, unroll=GATHER_UNROLL)


def _wait_row_gather(src_hbm, buf, sem, slot):
    pltpu.make_async_copy(src_hbm.at[pl.ds(0, buf.shape[1])], buf.at[slot], sem.at[slot]).wait()


def _pipelined_row_gather(src_hbm, idx_cur_ref, idx_nxt_ref, buf, sem, valid_ref):
    t = pl.program_id(0)
    nt = pl.num_programs(0)
    slot = lax.rem(t, 2)
    nxt = jnp.minimum(t + 1, nt - 1)

    @pl.when(jnp.logical_and(t == 0, valid_ref[0] != 0))
    def _():
        _start_row_gather(src_hbm, idx_cur_ref, buf, sem, 0)

    @pl.when(jnp.logical_and(t + 1 < nt, valid_ref[nxt] != 0))
    def _():
        _start_row_gather(src_hbm, idx_nxt_ref, buf, sem, 1 - slot)

    @pl.when(valid_ref[t] != 0)
    def _():
        _wait_row_gather(src_hbm, buf, sem, slot)

    return slot


def _expert_kernel(te_ref, tv_ref, idx_cur_ref, idx_nxt_ref, x_hbm, wgu_ref, bgu_ref, wd_ref, bd_ref, o_ref, xbuf, sem):
    t = pl.program_id(0)
    bm = xbuf.shape[1]
    half = xbuf.shape[2] * xbuf.shape[3]
    ff = wd_ref.shape[1]
    slot = _pipelined_row_gather(x_hbm, idx_cur_ref, idx_nxt_ref, xbuf, sem, tv_ref)

    @pl.when(tv_ref[t] != 0)
    def _():
        lo, hi = _unpack_bf16_pair(xbuf[slot].reshape(bm, half))
        hgu = _dot(lo.astype(BF16), wgu_ref[0, :half]) + _dot(hi.astype(BF16), wgu_ref[0, half:]) + bgu_ref[0]
        gate = jnp.minimum(hgu[:, :ff], SWIGLU_LIMIT)
        lin = jnp.clip(hgu[:, ff:], -SWIGLU_LIMIT, SWIGLU_LIMIT)
        act = gate * jax.nn.sigmoid(SWIGLU_ALPHA * gate) * (lin + 1.0)
        y = _dot(act.astype(BF16), wd_ref[0]) + bd_ref[0]
        o_ref[...] = _pack_bf16_pair(y[:, :half], y[:, half:]).reshape(o_ref.shape)

    @pl.when(tv_ref[t] == 0)
    def _():
        o_ref[...] = jnp.zeros(o_ref.shape, o_ref.dtype)


def _experts(v_rows, row_token, tile_expert, tile_valid, wgu, bgu, wd, bd, bm):
    _, s, _ = v_rows.shape
    rows = row_token.shape[0]
    e, d, ff2 = wgu.shape
    ff = ff2 // 2
    nt = rows // bm
    idx3 = row_token.reshape(nt, 1, bm)
    blocks = (_nbytes((bm, s, LANES), U32) + _nbytes((d, ff2), BF16) + _nbytes((ff, d), BF16)
              + _nbytes((1, ff2 + d), F32))
    temps = (2 * _nbytes((bm, s, LANES), U32) + 2 * _nbytes((bm, d), F32) + 2 * _nbytes((bm, d), BF16)
             + 3 * _nbytes((bm, ff2), F32))
    grid_spec = pltpu.PrefetchScalarGridSpec(
        num_scalar_prefetch=2,
        grid=(nt,),
        in_specs=[
            pl.BlockSpec((1, 1, bm), lambda t, te, tv: (t, 0, 0), memory_space=pltpu.SMEM),
            pl.BlockSpec((1, 1, bm), lambda t, te, tv: (jnp.minimum(t + 1, nt - 1), 0, 0), memory_space=pltpu.SMEM),
            pl.BlockSpec(memory_space=pl.ANY),
            pl.BlockSpec((1, d, ff2), lambda t, te, tv: (te[t], 0, 0)),
            pl.BlockSpec((1, 1, ff2), lambda t, te, tv: (te[t], 0, 0)),
            pl.BlockSpec((1, ff, d), lambda t, te, tv: (te[t], 0, 0)),
            pl.BlockSpec((1, 1, d), lambda t, te, tv: (te[t], 0, 0)),
        ],
        out_specs=pl.BlockSpec((bm, s, LANES), lambda t, te, tv: (t, 0, 0)),
        scratch_shapes=[pltpu.VMEM((2, bm, s, LANES), U32), pltpu.SemaphoreType.DMA((2,))],
    )
    return pl.pallas_call(
        _expert_kernel,
        grid_spec=grid_spec,
        out_shape=jax.ShapeDtypeStruct((rows, s, LANES), U32),
        compiler_params=_params(("arbitrary",), blocks, temps),
        name="experts",
    )(tile_expert, tile_valid, idx3, idx3, v_rows, wgu, bgu.reshape(e, 1, ff2), wd, bd.reshape(e, 1, d))


def _combine_kernel(tv_ref, idx_cur_ref, idx_nxt_ref, y_hbm, wt_ref, h_ref, g_ref, gf_ref, o_ref, ybuf, sem, *, final):
    bm = h_ref.shape[0]
    half = h_ref.shape[1] // 2
    slot = _pipelined_row_gather(y_hbm, idx_cur_ref, idx_nxt_ref, ybuf, sem, tv_ref)
    f_lo = None
    f_hi = None
    for k in range(TOP_K):
        lo, hi = _unpack_bf16_pair(ybuf[slot, k * bm:(k + 1) * bm].reshape(bm, half))
        w = wt_ref[:, k:k + 1]
        f_lo = lo * w if f_lo is None else f_lo + lo * w
        f_hi = hi * w if f_hi is None else f_hi + hi * w
    o_lo = h_ref[:, :half] + g_ref[0, :, :half] * f_lo
    o_hi = h_ref[:, half:] + g_ref[0, :, half:] * f_hi
    if final:
        ms = (jnp.sum(o_lo * o_lo, axis=-1, keepdims=True) + jnp.sum(o_hi * o_hi, axis=-1, keepdims=True)) / (2 * half)
        inv = lax.rsqrt(ms + NORM_EPS)
        o_lo = o_lo * inv * gf_ref[:, :half]
        o_hi = o_hi * inv * gf_ref[:, half:]
    o_ref[:, :half] = o_lo
    o_ref[:, half:] = o_hi


def _combine(y_rows, dest, wts2, h2, mod3, gate_blk, gain_final, rows, n, final):
    d = h2.shape[1]
    s = y_rows.shape[1]
    bm = _pick(n, 256)
    per_batch = n // bm
    nt = rows // bm
    idx3 = dest.reshape(nt, bm, TOP_K).transpose(0, 2, 1).reshape(nt, 1, TOP_K * bm)
    all_valid = jnp.ones((nt,), jnp.int32)
    blocks = _nbytes((bm, LANES), F32) + 2 * _nbytes((bm, d), F32) + 2 * _nbytes((1, d), F32)
    temps = 2 * _nbytes((TOP_K * bm, s, LANES), U32) + 6 * _nbytes((bm, d), F32)
    grid_spec = pltpu.PrefetchScalarGridSpec(
        num_scalar_prefetch=1,
        grid=(nt,),
        in_specs=[
            pl.BlockSpec((1, 1, TOP_K * bm), lambda i, tv: (i, 0, 0), memory_space=pltpu.SMEM),
            pl.BlockSpec((1, 1, TOP_K * bm), lambda i, tv: (jnp.minimum(i + 1, nt - 1), 0, 0),
                         memory_space=pltpu.SMEM),
            pl.BlockSpec(memory_space=pl.ANY),
            pl.BlockSpec((bm, LANES), lambda i, tv: (i, 0)),
            pl.BlockSpec((bm, d), lambda i, tv: (i, 0)),
            pl.BlockSpec((1, 1, d), lambda i, tv: (i // per_batch, 0, gate_blk)),
            pl.BlockSpec((1, d), lambda i, tv: (0, 0)),
        ],
        out_specs=pl.BlockSpec((bm, d), lambda i, tv: (i, 0)),
        scratch_shapes=[pltpu.VMEM((2, TOP_K * bm, s, LANES), U32), pltpu.SemaphoreType.DMA((2,))],
    )
    return pl.pallas_call(
        functools.partial(_combine_kernel, final=final),
        grid_spec=grid_spec,
        out_shape=jax.ShapeDtypeStruct((rows, d), F32),
        compiler_params=_params(("arbitrary",), blocks, temps),
        name="moe_combine",
    )(all_valid, idx3, idx3, y_rows, wts2, h2, mod3, gain_final.reshape(1, d))


def _rope_tables(n, rot_dim):
    t = jnp.arange(n, dtype=jnp.int32)
    row = (t // GRID_W).astype(F32)
    col = (t % GRID_W).astype(F32)
    half = rot_dim // 2
    inv = 1.0 / (ROPE_THETA ** (jnp.arange(0, half, 2, dtype=F32) / half))
    ang_r = row[:, None] * inv[None, :]
    ang_c = col[:, None] * inv[None, :]
    cos = jnp.concatenate([jnp.cos(ang_r), jnp.cos(ang_r), jnp.cos(ang_c), jnp.cos(ang_c)], axis=1)
    sin = jnp.concatenate([-jnp.sin(ang_r), jnp.sin(ang_r), -jnp.sin(ang_c), jnp.sin(ang_c)], axis=1)
    pad = LANES - rot_dim
    cos = jnp.pad(cos, ((0, 0), (0, pad)), constant_values=1.0)
    sin = jnp.pad(sin, ((0, 0), (0, pad)))
    return jnp.stack([cos, jnp.ones_like(cos)]), jnp.stack([sin, jnp.zeros_like(sin)])


def _dft_cos_sin(n):
    f = 64 if n % 64 == 0 and n > 64 else 1
    m = jnp.arange(n, dtype=jnp.int32)

    def table(rows):
        ang = ((rows[:, None] * m[None, :]) % n).astype(F32) * (2.0 * math.pi / n)
        return jnp.cos(ang), jnp.sin(ang)

    ca, sa = table(jnp.arange(n // f, dtype=jnp.int32) * f)
    if f == 1:
        return ca, sa
    cb, sb = table(jnp.arange(f, dtype=jnp.int32))
    ca, sa, cb, sb = ca[:, None, :], sa[:, None, :], cb[None, :, :], sb[None, :, :]
    return (ca * cb - sa * sb).reshape(n, n), (sa * cb + ca * sb).reshape(n, n)


def _deinterleave_kernel(w_ref, p_ref, o_ref):
    cw = p_ref.shape[0]
    ff = o_ref.shape[2] // 2
    for j in range(w_ref.shape[2] // cw):
        y = _dot(w_ref[0, :, j * cw:(j + 1) * cw].astype(BF16), p_ref[...]).astype(o_ref.dtype)
        o_ref[0, :, j * (cw // 2):(j + 1) * (cw // 2)] = y[:, :cw // 2]
        o_ref[0, :, ff + j * (cw // 2):ff + (j + 1) * (cw // 2)] = y[:, cw // 2:]


def _deinterleave_cast(w):
    e, d, ff2 = w.shape
    cw = min(512, ff2)
    assert ff2 % cw == 0 and (cw // 2) % LANES == 0
    src = jnp.arange(cw, dtype=jnp.int32)
    dst = jnp.where(src % 2 == 0, src // 2, cw // 2 + src // 2)
    perm = (dst[:, None] == jnp.arange(cw, dtype=jnp.int32)[None, :]).astype(BF16)
    bk = _pick(d, 1024)
    blocks = _nbytes((bk, ff2), F32) + _nbytes((cw, cw), BF16) + _nbytes((bk, ff2), BF16)
    return pl.pallas_call(
        _deinterleave_kernel,
        grid=(e, d // bk),
        in_specs=[
            pl.BlockSpec((1, bk, ff2), lambda i, j: (i, j, 0)),
            pl.BlockSpec((cw, cw), lambda i, j: (0, 0)),
        ],
        out_specs=pl.BlockSpec((1, bk, ff2), lambda i, j: (i, j, 0)),
        out_shape=jax.ShapeDtypeStruct((e, d, ff2), BF16),
        compiler_params=_params(("arbitrary", "arbitrary"), blocks, 3 * _nbytes((bk, ff2), BF16)),
        name="deinterleave_cast",
    )(w, perm)


def _fourier_matrices(n, n_sub):
    cc, sc = _dft_cos_sin(FOURIER_GROUP_W)
    w_chan = (jnp.concatenate([cc, sc], axis=1) * FOURIER_GROUP_W ** -0.5).astype(BF16)
    cn, sn = _dft_cos_sin(n)
    lat = jnp.concatenate([cn, -sn], axis=1) * n ** -0.5
    m = n // n_sub
    cm, sm = _dft_cos_sin(m)
    eye = jnp.eye(n_sub, dtype=F32)
    ctx = jnp.concatenate([jnp.kron(eye, cm), -jnp.kron(eye, sm)], axis=1) * m ** -0.5
    return w_chan, jnp.stack([lat, ctx]).astype(BF16)


def _layer_weights(l, w_in, mla_w_uq, mla_w_ukv, w_br_gqa, w_br_mla, w_br_fourier, w_out, w_router, b_router,
                   w_gate_up, b_gate_up, w_down):
    d = w_in.shape[1]
    wi = w_in[l]
    kv_cols = 2 * GQA_KV_W + MLA_KV_LORA + MLA_ROPE_DIM
    w_kv = jnp.pad(wi[:, :kv_cols], ((0, 0), (0, KV_W - kv_cols))).astype(BF16)
    w_main = wi[:, kv_cols:].astype(BF16)
    uq = mla_w_uq[l].reshape(MLA_Q_LORA, MLA_HEADS, MLA_QK_DIM)
    uq_rope = jnp.pad(uq[:, :, MLA_NOPE_DIM:], ((0, 0), (0, 0), (0, LANES - MLA_ROPE_DIM)))
    w_uq = jnp.concatenate([uq[:, :, :MLA_NOPE_DIM].reshape(MLA_Q_LORA, -1), uq_rope.reshape(MLA_Q_LORA, -1)],
                           axis=1).astype(BF16)
    ukv = mla_w_ukv[l].reshape(MLA_KV_LORA, MLA_HEADS, MLA_NOPE_DIM + MLA_V_DIM)
    w_ukv = jnp.concatenate([ukv[:, :, :MLA_NOPE_DIM].reshape(MLA_KV_LORA, -1),
                             ukv[:, :, MLA_NOPE_DIM:].reshape(MLA_KV_LORA, -1)], axis=1).astype(BF16)
    e = w_router.shape[2]
    wr = jnp.pad(w_router[l], ((0, 0), (0, LANES - e))).astype(BF16)
    br = jnp.pad(b_router[l], (0, LANES - e), constant_values=NEG_BIG).reshape(1, LANES)
    wgu = _deinterleave_cast(w_gate_up[l])
    bgu = jnp.concatenate([b_gate_up[l][..., 0::2], b_gate_up[l][..., 1::2]], axis=-1)
    return dict(w_kv=w_kv, w_main=w_main, w_uq=w_uq, w_ukv=w_ukv, w_br_gqa=w_br_gqa[l].astype(BF16),
                w_br_mla=w_br_mla[l].astype(BF16), w_br_fourier=w_br_fourier[l].astype(BF16),
                w_out=w_out[l].astype(BF16), wr=wr, br=br, wgu=wgu, bgu=bgu, wd=w_down[l].astype(BF16))


def _route(idx, n_experts, bm):
    pairs = idx.shape[0] * TOP_K
    flat_e = idx.reshape(pairs)
    onehot = (flat_e[:, None] == jnp.arange(n_experts, dtype=jnp.int32)[None, :]).astype(jnp.int32)
    csum = jnp.cumsum(onehot, axis=0)
    counts = csum[-1]
    tiles_e = (counts + bm - 1) // bm
    tile_end = jnp.cumsum(tiles_e)
    tile_start = tile_end - tiles_e
    dest = jnp.sum(onehot * (csum - 1 + tile_start[None, :] * bm), axis=1)
    n_tiles = pairs // bm + n_experts
    rows = n_tiles * bm
    row_token = jnp.zeros((rows,), jnp.int32).at[dest].set(jnp.arange(pairs, dtype=jnp.int32) // TOP_K)
    tile_ids = jnp.arange(n_tiles, dtype=jnp.int32)
    tile_expert = jnp.minimum(jnp.sum((tile_ids[:, None] >= tile_end[None, :]).astype(jnp.int32), axis=1),
                              n_experts - 1)
    tile_valid = (tile_ids < tile_end[-1]).astype(jnp.int32)
    return row_token, dest, tile_expert, tile_valid


def _layer(h_all, mod3, lw, gains, tabs, fmats, n_lat, ctx_len, last, norm_final):
    nb_all, n, d = h_all.shape
    nb = n_lat if last else nb_all
    rows = nb * n
    cg, sg, cm, sm = tabs
    w_chan, w_pos = fmats

    u = _norm_mod(h_all, gains["norm_mix"], mod3, 0, 1)
    u2 = u.reshape(nb_all * n, d)
    z_kv = _matmul(u2, lw["w_kv"], nb_all * n).reshape(nb_all, n, KV_W)
    z_main2 = _matmul(u2, lw["w_main"], rows)
    z_main = z_main2.reshape(nb, n, -1)
    ka, kvb, kr = _prep_kv(z_kv, gains["gqa_k_norm"], gains["mla_kv_norm"], lw["w_ukv"], (cg, sg, cm, sm), n_lat)
    qa, qb = _prep_q(z_main, nb, gains["gqa_q_norm"], gains["mla_q_norm"], lw["w_uq"], (cg, sg, cm, sm), n_lat)
    y_a = _gqa_attention(qa, ka, z_kv, nb, n_lat, ctx_len)
    y_b = _mla_attention(qb, kvb, kr, nb, n_lat, ctx_len)
    y_c = _fourier(z_main, nb, n_lat, w_chan, w_pos)
    m = _merge(y_a.reshape(rows, -1), y_b.reshape(rows, -1), y_c.reshape(rows, -1),
               lw["w_br_gqa"], lw["w_br_mla"], lw["w_br_fourier"], z_main2, rows, d)
    h2 = _out_proj(m, lw["w_out"], h_all.reshape(nb_all * n, d), mod3, 2, rows, n)

    v_packed, idx, wts = _ffn_pre(h2.reshape(nb, n, d), nb, gains["norm_ffn"], mod3, 3, 4, lw["wr"], lw["br"])
    n_experts = lw["wgu"].shape[0]
    bm = 256
    row_token, dest, tile_expert, tile_valid = _route(idx.reshape(rows, LANES)[:, :TOP_K], n_experts, bm)
    y_rows = _experts(v_packed, row_token, tile_expert, tile_valid, lw["wgu"], lw["bgu"], lw["wd"], lw["bd"], bm)
    out = _combine(y_rows, dest, wts.reshape(rows, LANES), h2, mod3, 5, norm_final, rows, n, last)
    return out.reshape(nb, n, d)


def kernel(x, c, ctx, c_ctx, w_ada, b_ada, norm_mix, w_in, gqa_q_norm, gqa_k_norm, mla_q_norm, mla_w_uq, mla_kv_norm, mla_w_ukv, w_br_gqa, w_br_mla, w_br_fourier, w_out, norm_ffn, w_router, b_router, w_gate_up, b_gate_up, w_down, b_down, norm_final):
    n_lat, n, d = x.shape
    ctx_len = ctx.shape[1]
    depth = w_ada.shape[0]
    assert n_lat * ctx_len == n, "context tokens must fill exactly one extra batch row"
    assert n % GRID_W == 0 and ctx_len % 8 == 0

    tabs_g = _rope_tables(n, HEAD_DIM)
    tabs_m = _rope_tables(n, MLA_ROPE_DIM)
    tabs = (tabs_g[0], tabs_g[1], tabs_m[0], tabs_m[1])
    fmats = _fourier_matrices(n, n_lat)

    mod_rows = 16
    c_all = jnp.zeros((mod_rows, d), F32).at[:n_lat].set(c).at[n_lat].set(c_ctx)
    mod = _ada_mod(c_all, w_ada, b_ada)

    h_all = jnp.concatenate([x, ctx.reshape(1, n, d)], axis=0)
    for l in range(depth):
        last = l == depth - 1
        lw = _layer_weights(l, w_in, mla_w_uq, mla_w_ukv, w_br_gqa, w_br_mla, w_br_fourier, w_out, w_router,
                            b_router, w_gate_up, b_gate_up, w_down)
        lw["bd"] = b_down[l]
        gains = dict(norm_mix=norm_mix[l], gqa_q_norm=gqa_q_norm[l], gqa_k_norm=gqa_k_norm[l],
                     mla_q_norm=mla_q_norm[l], mla_kv_norm=mla_kv_norm[l], norm_ffn=norm_ffn[l])
        mod3 = mod[l].reshape(mod_rows, 1, 6 * d)
        h_all = _layer(h_all, mod3, lw, gains, tabs, fmats, n_lat, ctx_len, last, norm_final)
    return h_all
```

```python
import functools
import math

import jax
import jax.numpy as jnp
from jax import lax
from jax.experimental import pallas as pl
from jax.experimental.pallas import tpu as pltpu

F32 = jnp.float32
BF16 = jnp.bfloat16
U32 = jnp.uint32

GRID_W = 64
ROPE_THETA = 10000.0
NORM_EPS = 1e-6
HEAD_DIM = 128
GQA_HEADS = 16
GQA_KV_HEADS = 4
GQA_GROUP = GQA_HEADS // GQA_KV_HEADS
GQA_Q_W = GQA_HEADS * HEAD_DIM
GQA_KV_W = GQA_KV_HEADS * HEAD_DIM
GQA_SCALE = HEAD_DIM ** -0.5
MLA_HEADS = 8
MLA_Q_LORA = 1024
MLA_KV_LORA = 512
MLA_NOPE_DIM = 128
MLA_ROPE_DIM = 64
MLA_V_DIM = 128
MLA_QK_DIM = MLA_NOPE_DIM + MLA_ROPE_DIM
MLA_OUT_W = MLA_HEADS * MLA_V_DIM
MLA_SCALE = MLA_QK_DIM ** -0.5
FOURIER_GROUPS = 4
FOURIER_GROUP_W = 256
FOURIER_W = FOURIER_GROUPS * FOURIER_GROUP_W
TOP_K = 4
SWIGLU_ALPHA = 1.702
SWIGLU_LIMIT = 7.0
LOG2E = math.log2(math.e)
GQA_Q_SCALE = GQA_SCALE * LOG2E
MLA_Q_SCALE = MLA_SCALE * LOG2E

LANES = 128
V7X_VMEM_BYTES = 64 * 2 ** 20
VMEM_CAP = V7X_VMEM_BYTES - 6 * 2 ** 20
VMEM_FLOOR = 32 * 2 ** 20
VMEM_INTERNAL = 8 * 2 ** 20

KV_W = 2048
KV_OFF_K, KV_OFF_V, KV_OFF_CKV, KV_OFF_KR = 0, 512, 1024, 1536
MAIN_OFF_QA, MAIN_OFF_CQ, MAIN_OFF_F, MAIN_OFF_G = 0, 2048, 3072, 4096
NEG_BIG = -1e30


def _vmem(block_bytes, temp_bytes=0):
    return int(min(VMEM_CAP, max(VMEM_FLOOR, 2 * block_bytes + temp_bytes + VMEM_INTERNAL)))


def _params(sem, block_bytes, temp_bytes=0):
    return pltpu.CompilerParams(dimension_semantics=sem, vmem_limit_bytes=_vmem(block_bytes, temp_bytes))


def _nbytes(shape, dtype):
    return math.prod(shape) * jnp.dtype(dtype).itemsize


def _pick(n, target):
    if n <= target:
        return n
    for cand in range(target, 7, -1):
        if n % cand == 0 and cand % 8 == 0:
            return cand
    return n


def _rms(x, gain):
    return x * lax.rsqrt(jnp.mean(x * x, axis=-1, keepdims=True) + NORM_EPS) * gain


def _rope(x, cos, sin, blk):
    lane = lax.broadcasted_iota(jnp.int32, x.shape, x.ndim - 1)
    first = (lane % (2 * blk)) < blk
    swapped = jnp.where(first, pltpu.roll(x, LANES - blk, x.ndim - 1), pltpu.roll(x, blk, x.ndim - 1))
    return x * cos + swapped * sin


def _pack_bf16_pair(lo, hi):
    lo_bits = lax.bitcast_convert_type(lo.astype(BF16).astype(F32), U32) >> 16
    hi_bits = lax.bitcast_convert_type(hi.astype(BF16).astype(F32), U32) & jnp.uint32(0xFFFF0000)
    return lo_bits | hi_bits


def _unpack_bf16_pair(word):
    lo = lax.bitcast_convert_type(word << 16, F32)
    hi = lax.bitcast_convert_type(word & jnp.uint32(0xFFFF0000), F32)
    return lo, hi


def _dot(a, b):
    return jnp.dot(a, b, preferred_element_type=F32)


def _dot_nt(a, b):
    return lax.dot_general(a, b, (((1,), (1,)), ((), ())), preferred_element_type=F32)


def _ada_kernel(c_ref, w_ref, b_ref, o_ref):
    c = c_ref[...]
    act = (c * jax.nn.sigmoid(c)).astype(BF16)
    o_ref[0] = _dot(act, w_ref[0].astype(BF16)) + b_ref[0]


def _ada_mod(c_all, w_ada, b_ada):
    depth, d, n6 = w_ada.shape
    rows = c_all.shape[0]
    bn = _pick(n6, 512)
    blocks = _nbytes((rows, d), F32) + _nbytes((d, bn), F32) + _nbytes((rows, bn), F32) * 2
    return pl.pallas_call(
        _ada_kernel,
        grid=(depth, n6 // bn),
        in_specs=[
            pl.BlockSpec((rows, d), lambda l, j: (0, 0)),
            pl.BlockSpec((1, d, bn), lambda l, j: (l, 0, j)),
            pl.BlockSpec((1, 1, bn), lambda l, j: (l, 0, j)),
        ],
        out_specs=pl.BlockSpec((1, rows, bn), lambda l, j: (l, 0, j)),
        out_shape=jax.ShapeDtypeStruct((depth, rows, n6), F32),
        compiler_params=_params(("arbitrary", "arbitrary"), blocks, _nbytes((d, bn), BF16)),
        name="adaln_mod",
    )(c_all, w_ada, b_ada.reshape(depth, 1, n6))


def _norm_mod_kernel(x_ref, g_ref, sh_ref, sc_ref, o_ref):
    y = _rms(x_ref[0], g_ref[...])
    o_ref[0] = (y * (1.0 + sc_ref[0]) + sh_ref[0]).astype(o_ref.dtype)


def _norm_mod(h, gain, mod3, sh_blk, sc_blk):
    nb, n, d = h.shape
    bn = _pick(n, 256)
    blocks = _nbytes((bn, d), F32) + _nbytes((bn, d), BF16) + 3 * _nbytes((1, d), F32)
    return pl.pallas_call(
        _norm_mod_kernel,
        grid=(nb, n // bn),
        in_specs=[
            pl.BlockSpec((1, bn, d), lambda b, i: (b, i, 0)),
            pl.BlockSpec((1, d), lambda b, i: (0, 0)),
            pl.BlockSpec((1, 1, d), lambda b, i: (b, 0, sh_blk)),
            pl.BlockSpec((1, 1, d), lambda b, i: (b, 0, sc_blk)),
        ],
        out_specs=pl.BlockSpec((1, bn, d), lambda b, i: (b, i, 0)),
        out_shape=jax.ShapeDtypeStruct((nb, n, d), BF16),
        compiler_params=_params(("arbitrary", "arbitrary"), blocks, 2 * _nbytes((bn, d), F32)),
        name="norm_mod",
    )(h, gain.reshape(1, d), mod3, mod3)


def _mm_kernel(x_ref, w_ref, o_ref):
    o_ref[...] = _dot(x_ref[...], w_ref[0]).astype(o_ref.dtype)


def _matmul(x2, w, layer, rows, out_dtype=BF16):
    k = x2.shape[1]
    n = w.shape[2]
    bm = _pick(rows, 1024)
    bn = _pick(n, 1024)
    blocks = _nbytes((bm, k), BF16) + _nbytes((k, bn), BF16) + _nbytes((bm, bn), out_dtype)
    return pl.pallas_call(
        _mm_kernel,
        grid=(rows // bm, n // bn),
        in_specs=[
            pl.BlockSpec((bm, k), lambda i, j: (i, 0)),
            pl.BlockSpec((1, k, bn), lambda i, j: (layer, 0, j)),
        ],
        out_specs=pl.BlockSpec((bm, bn), lambda i, j: (i, j)),
        out_shape=jax.ShapeDtypeStruct((rows, n), out_dtype),
        compiler_params=_params(("arbitrary", "arbitrary"), blocks, _nbytes((bm, bn), F32)),
        name="matmul",
    )(x2, w)


def _prep_kv_kernel(z_ref, gk_ref, gkv_ref, wukv_ref, cg_ref, sg_ref, cm_ref, sm_ref, ka_ref, kvb_ref, kr_ref):
    gk = gk_ref[...]
    cg, sg = cg_ref[0], sg_ref[0]
    for h in range(GQA_KV_HEADS):
        sl = slice(KV_OFF_K + h * HEAD_DIM, KV_OFF_K + (h + 1) * HEAD_DIM)
        y = _rope(_rms(z_ref[0, :, sl].astype(F32), gk), cg, sg, HEAD_DIM // 4)
        ka_ref[0, :, h * HEAD_DIM:(h + 1) * HEAD_DIM] = y.astype(ka_ref.dtype)
    ckv = _rms(z_ref[0, :, KV_OFF_CKV:KV_OFF_CKV + MLA_KV_LORA].astype(F32), gkv_ref[...])
    kvb_ref[0] = _dot(ckv.astype(BF16), wukv_ref[...]).astype(kvb_ref.dtype)
    kr = z_ref[0, :, KV_OFF_KR:KV_OFF_KR + LANES].astype(F32)
    kr_ref[0] = _rope(kr, cm_ref[0], sm_ref[0], MLA_ROPE_DIM // 4).astype(kr_ref.dtype)


def _prep_kv(z_kv, gk, gkv, wukv, tabs, n_lat):
    nb, n, _ = z_kv.shape
    bn = _pick(n, 512)
    kvw = wukv.shape[1]
    tab = pl.BlockSpec((1, bn, LANES), lambda b, i: (b // n_lat, i, 0))
    blocks = (_nbytes((bn, KV_W), BF16) + _nbytes(wukv.shape, BF16) + 4 * _nbytes((bn, LANES), F32)
              + _nbytes((bn, GQA_KV_W + kvw + LANES), BF16))
    return pl.pallas_call(
        _prep_kv_kernel,
        grid=(nb, n // bn),
        in_specs=[
            pl.BlockSpec((1, bn, KV_W), lambda b, i: (b, i, 0)),
            pl.BlockSpec((1, HEAD_DIM), lambda b, i: (0, 0)),
            pl.BlockSpec((1, MLA_KV_LORA), lambda b, i: (0, 0)),
            pl.BlockSpec(wukv.shape, lambda b, i: (0, 0)),
            tab, tab, tab, tab,
        ],
        out_specs=[
            pl.BlockSpec((1, bn, GQA_KV_W), lambda b, i: (b, i, 0)),
            pl.BlockSpec((1, bn, kvw), lambda b, i: (b, i, 0)),
            pl.BlockSpec((1, bn, LANES), lambda b, i: (b, i, 0)),
        ],
        out_shape=[
            jax.ShapeDtypeStruct((nb, n, GQA_KV_W), BF16),
            jax.ShapeDtypeStruct((nb, n, kvw), BF16),
            jax.ShapeDtypeStruct((nb, n, LANES), BF16),
        ],
        compiler_params=_params(("arbitrary", "arbitrary"), blocks, 4 * _nbytes((bn, kvw), F32)),
        name="prep_kv",
    )(z_kv, gk.reshape(1, HEAD_DIM), gkv.reshape(1, MLA_KV_LORA), wukv, *tabs)


def _prep_q_kernel(zq_ref, zcq_ref, gq_ref, gcq_ref, wuq_ref, cg_ref, sg_ref, cm_ref, sm_ref, qa_ref, qb_ref):
    gq = gq_ref[...]
    cg, sg = cg_ref[0], sg_ref[0]
    for h in range(GQA_HEADS):
        sl = slice(h * HEAD_DIM, (h + 1) * HEAD_DIM)
        y = _rope(_rms(zq_ref[0, :, sl].astype(F32), gq), cg, sg, HEAD_DIM // 4)
        qa_ref[0, :, sl] = (y * GQA_Q_SCALE).astype(qa_ref.dtype)
    cq = _rms(zcq_ref[0].astype(F32), gcq_ref[...])
    q = _dot(cq.astype(BF16), wuq_ref[...])
    nope_w = MLA_HEADS * MLA_NOPE_DIM
    qb_ref[0, :, :nope_w] = (q[:, :nope_w] * MLA_Q_SCALE).astype(qb_ref.dtype)
    cm, sm = cm_ref[0], sm_ref[0]
    for h in range(MLA_HEADS):
        sl = slice(nope_w + h * LANES, nope_w + (h + 1) * LANES)
        qb_ref[0, :, sl] = (_rope(q[:, sl], cm, sm, MLA_ROPE_DIM // 4) * MLA_Q_SCALE).astype(qb_ref.dtype)


def _prep_q(z_main, nb, gq, gcq, wuq, tabs, n_lat):
    n = z_main.shape[1]
    bn = _pick(n, 256)
    qbw = wuq.shape[1]
    tab = pl.BlockSpec((1, bn, LANES), lambda b, i: (b // n_lat, i, 0))
    blocks = (_nbytes((bn, GQA_Q_W + MLA_Q_LORA), BF16) + _nbytes(wuq.shape, BF16)
              + 4 * _nbytes((bn, LANES), F32) + _nbytes((bn, GQA_Q_W + qbw), BF16))
    return pl.pallas_call(
        _prep_q_kernel,
        grid=(nb, n // bn),
        in_specs=[
            pl.BlockSpec((1, bn, GQA_Q_W), lambda b, i: (b, i, MAIN_OFF_QA // GQA_Q_W)),
            pl.BlockSpec((1, bn, MLA_Q_LORA), lambda b, i: (b, i, MAIN_OFF_CQ // MLA_Q_LORA)),
            pl.BlockSpec((1, HEAD_DIM), lambda b, i: (0, 0)),
            pl.BlockSpec((1, MLA_Q_LORA), lambda b, i: (0, 0)),
            pl.BlockSpec(wuq.shape, lambda b, i: (0, 0)),
            tab, tab, tab, tab,
        ],
        out_specs=[
            pl.BlockSpec((1, bn, GQA_Q_W), lambda b, i: (b, i, 0)),
            pl.BlockSpec((1, bn, qbw), lambda b, i: (b, i, 0)),
        ],
        out_shape=[
            jax.ShapeDtypeStruct((nb, n, GQA_Q_W), BF16),
            jax.ShapeDtypeStruct((nb, n, qbw), BF16),
        ],
        compiler_params=_params(("arbitrary", "arbitrary"), blocks, 4 * _nbytes((bn, qbw), F32)),
        name="prep_q",
    )(z_main, z_main, gq.reshape(1, HEAD_DIM), gcq.reshape(1, MLA_Q_LORA), wuq, *tabs)


def _online_softmax_pv(q, chunks):
    m = None
    acc = None
    for k, v, keep in chunks:
        v1 = jnp.concatenate([v, jnp.ones(v.shape, v.dtype)], axis=1)
        s = _dot_nt(q, k)
        if keep is not None:
            s = jnp.where(keep, s, NEG_BIG)
        mc = s.max(axis=-1, keepdims=True)
        if m is None:
            m = mc
            acc = _dot(jnp.exp2(s - m).astype(v.dtype), v1)
        else:
            m_new = jnp.maximum(m, mc)
            acc = acc * jnp.exp2(m - m_new) + _dot(jnp.exp2(s - m_new).astype(v.dtype), v1)
            m = m_new
    return acc[:, :HEAD_DIM] / acc[:, HEAD_DIM:HEAD_DIM + 1]


def _key_chunks(n, width):
    return [(c, min(width, n - c)) for c in range(0, n, width)]


GQA_KEY_CHUNK = 256
MLA_KEY_CHUNK = 512


def _gqa_kernel(q_ref, kc_ref, kl_ref, vc_ref, vl_ref, o_ref, *, n_lat, has_ctx_batch):
    bq = q_ref.shape[1]
    q = q_ref[0]
    qs = jnp.concatenate([q[:, g * HEAD_DIM:(g + 1) * HEAD_DIM] for g in range(GQA_GROUP)], axis=0)

    def finish(chunks):
        o = _online_softmax_pv(qs, chunks)
        for g in range(GQA_GROUP):
            o_ref[0, :, g * HEAD_DIM:(g + 1) * HEAD_DIM] = o[g * bq:(g + 1) * bq].astype(o_ref.dtype)

    def context():
        finish([(kc_ref[0], vc_ref[0], None)])

    def latent():
        finish([(kc_ref[0], vc_ref[0], None)]
               + [(kl_ref[0, c:c + w], vl_ref[0, c:c + w], None) for c, w in _key_chunks(kl_ref.shape[1], GQA_KEY_CHUNK)])

    if has_ctx_batch:
        b = pl.program_id(0)
        pl.when(b < n_lat)(latent)
        pl.when(b == n_lat)(context)
    else:
        latent()


def _mla_kernel(*refs, n_lat, ctx_len, has_ctx_batch):
    if has_ctx_batch:
        qn_ref, qr_ref, knc_ref, krc_ref, knl_ref, krl_ref, vc_ref, vl_ref, kns_ref, krs_ref, vs_ref, o_ref = refs
    else:
        qn_ref, qr_ref, knc_ref, krc_ref, knl_ref, krl_ref, vc_ref, vl_ref, o_ref = refs
    bq = qn_ref.shape[1]
    q = jnp.concatenate([qn_ref[0], qr_ref[0]], axis=1)

    def keys(kn_ref, kr_ref, c, w):
        return jnp.concatenate([kn_ref[0, c:c + w], kr_ref[0, c:c + w]], axis=1)

    def finish(chunks):
        o_ref[0] = _online_softmax_pv(q, chunks).astype(o_ref.dtype)

    def latent():
        finish([(keys(knc_ref, krc_ref, 0, ctx_len), vc_ref[0], None)]
               + [(keys(knl_ref, krl_ref, c, w), vl_ref[0, c:c + w], None)
                  for c, w in _key_chunks(knl_ref.shape[1], MLA_KEY_CHUNK)])

    def context():
        row = lax.broadcasted_iota(jnp.int32, (bq, bq), 0) // ctx_len
        col = lax.broadcasted_iota(jnp.int32, (bq, bq), 1) // ctx_len
        finish([(keys(kns_ref, krs_ref, 0, bq), vs_ref[0], row == col if bq > ctx_len else None)])

    if has_ctx_batch:
        b = pl.program_id(0)
        pl.when(b < n_lat)(latent)
        pl.when(b == n_lat)(context)
    else:
        latent()


def _ctx_idx(b, i, n_lat):
    return jnp.where(b < n_lat, b, i)


def _lat_idx(b, n_lat):
    return jnp.where(b < n_lat, b, 0)


def _gqa_attention(qa, ka, z_kv, nb, n_lat, ctx_len):
    n = qa.shape[1]
    bq = ctx_len
    qw = GQA_GROUP * HEAD_DIM
    v_blk = KV_OFF_V // HEAD_DIM
    kern = functools.partial(_gqa_kernel, n_lat=n_lat, has_ctx_batch=nb > n_lat)
    blocks = (2 * _nbytes((bq, qw), BF16) + 2 * _nbytes((ctx_len + n, HEAD_DIM), BF16))
    temps = 6 * _nbytes((GQA_GROUP * bq, max(ctx_len, GQA_KEY_CHUNK)), F32)
    return pl.pallas_call(
        kern,
        grid=(nb, GQA_KV_HEADS, n // bq),
        in_specs=[
            pl.BlockSpec((1, bq, qw), lambda b, h, i: (b, i, h)),
            pl.BlockSpec((1, ctx_len, HEAD_DIM), lambda b, h, i: (n_lat, _ctx_idx(b, i, n_lat), h)),
            pl.BlockSpec((1, n, HEAD_DIM), lambda b, h, i: (_lat_idx(b, n_lat), 0, h)),
            pl.BlockSpec((1, ctx_len, HEAD_DIM), lambda b, h, i: (n_lat, _ctx_idx(b, i, n_lat), v_blk + h)),
            pl.BlockSpec((1, n, HEAD_DIM), lambda b, h, i: (_lat_idx(b, n_lat), 0, v_blk + h)),
        ],
        out_specs=pl.BlockSpec((1, bq, qw), lambda b, h, i: (b, i, h)),
        out_shape=jax.ShapeDtypeStruct((nb, n, GQA_Q_W), BF16),
        compiler_params=_params(("arbitrary", "arbitrary", "arbitrary"), blocks, temps),
        name="gqa_attention",
    )(qa, ka, ka, z_kv, z_kv)


def _mla_attention(qb, kvb, kr, nb, n_lat, ctx_len):
    n = qb.shape[1]
    bq = ctx_len * max(1, min(n, 1024) // ctx_len)
    assert n % bq == 0
    has_ctx_batch = nb > n_lat
    kern = functools.partial(_mla_kernel, n_lat=n_lat, ctx_len=ctx_len, has_ctx_batch=has_ctx_batch)
    blocks = 6 * _nbytes((bq, LANES), BF16) + 3 * _nbytes((ctx_len + n, LANES), BF16)
    temps = 6 * _nbytes((bq, max(bq, MLA_KEY_CHUNK)), F32) + 4 * _nbytes((n, 2 * LANES), BF16)
    ctx_spec = lambda col: pl.BlockSpec((1, ctx_len, LANES), lambda b, h, i: (n_lat, _lat_idx(b, n_lat), col(h)))
    lat_spec = lambda col: pl.BlockSpec((1, n, LANES), lambda b, h, i: (_lat_idx(b, n_lat), 0, col(h)))
    self_spec = lambda col: pl.BlockSpec((1, bq, LANES), lambda b, h, i: (n_lat, jnp.where(b < n_lat, 0, i), col(h)))
    in_specs = [
        pl.BlockSpec((1, bq, LANES), lambda b, h, i: (b, i, h)),
        pl.BlockSpec((1, bq, LANES), lambda b, h, i: (b, i, MLA_HEADS + h)),
        ctx_spec(lambda h: h), ctx_spec(lambda h: 0),
        lat_spec(lambda h: h), lat_spec(lambda h: 0),
        ctx_spec(lambda h: MLA_HEADS + h), lat_spec(lambda h: MLA_HEADS + h),
    ]
    operands = [qb, qb, kvb, kr, kvb, kr, kvb, kvb]
    if has_ctx_batch:
        in_specs += [self_spec(lambda h: h), self_spec(lambda h: 0), self_spec(lambda h: MLA_HEADS + h)]
        operands += [kvb, kr, kvb]
    return pl.pallas_call(
        kern,
        grid=(nb, MLA_HEADS, n // bq),
        in_specs=in_specs,
        out_specs=pl.BlockSpec((1, bq, LANES), lambda b, h, i: (b, i, h)),
        out_shape=jax.ShapeDtypeStruct((nb, n, MLA_OUT_W), BF16),
        compiler_params=_params(("arbitrary", "arbitrary", "arbitrary"), blocks, temps),
        name="mla_attention",
    )(*operands)


def _fourier_kernel(x_ref, wc_ref, wp_ref, o_ref):
    gw = x_ref.shape[2]
    xcs = _dot(x_ref[0], wc_ref[...]).astype(BF16)
    stacked = jnp.concatenate([xcs[:, :gw], xcs[:, gw:]], axis=0)
    o_ref[0] = _dot(wp_ref[0], stacked).astype(o_ref.dtype)


def _fourier(z_main, nb, n_lat, w_chan, w_pos):
    n = z_main.shape[1]
    gw = FOURIER_GROUP_W
    blocks = 2 * _nbytes((n, gw), BF16) + _nbytes((gw, 2 * gw), BF16) + _nbytes((n, 2 * n), BF16)
    temps = _nbytes((n, 2 * gw), F32) + 2 * _nbytes((2 * n, gw), BF16) + _nbytes((n, gw), F32)
    return pl.pallas_call(
        _fourier_kernel,
        grid=(nb, FOURIER_GROUPS),
        in_specs=[
            pl.BlockSpec((1, n, gw), lambda b, g: (b, 0, MAIN_OFF_F // gw + g)),
            pl.BlockSpec((gw, 2 * gw), lambda b, g: (0, 0)),
            pl.BlockSpec((1, n, 2 * n), lambda b, g: (b // n_lat, 0, 0)),
        ],
        out_specs=pl.BlockSpec((1, n, gw), lambda b, g: (b, 0, g)),
        out_shape=jax.ShapeDtypeStruct((nb, n, FOURIER_W), BF16),
        compiler_params=_params(("arbitrary", "arbitrary"), blocks, temps),
        name="fourier_mix",
    )(z_main, w_chan, w_pos)


def _merge_kernel(ya_ref, yb_ref, yc_ref, wa_ref, wb_ref, wc_ref, ga_ref, gb_ref, gc_ref, o_ref):
    m = jax.nn.sigmoid(ga_ref[...].astype(F32)) * _dot(ya_ref[...], wa_ref[0])
    m = m + jax.nn.sigmoid(gb_ref[...].astype(F32)) * _dot(yb_ref[...], wb_ref[0])
    m = m + jax.nn.sigmoid(gc_ref[...].astype(F32)) * _dot(yc_ref[...], wc_ref[0])
    o_ref[...] = m.astype(o_ref.dtype)


def _merge(ya, yb, yc, wa, wb, wc, layer, z_main2, rows, d):
    bm = _pick(rows, 512)
    bn = _pick(d, 1024)
    g0 = MAIN_OFF_G // bn
    gstep = d // bn
    kin = ya.shape[1] + yb.shape[1] + yc.shape[1]
    blocks = _nbytes((bm, kin), BF16) + _nbytes((kin, bn), BF16) + 4 * _nbytes((bm, bn), BF16)
    gate = lambda k: pl.BlockSpec((bm, bn), lambda i, j: (i, g0 + k * gstep + j))
    return pl.pallas_call(
        _merge_kernel,
        grid=(rows // bm, d // bn),
        in_specs=[
            pl.BlockSpec((bm, ya.shape[1]), lambda i, j: (i, 0)),
            pl.BlockSpec((bm, yb.shape[1]), lambda i, j: (i, 0)),
            pl.BlockSpec((bm, yc.shape[1]), lambda i, j: (i, 0)),
            pl.BlockSpec((1, wa.shape[1], bn), lambda i, j: (layer, 0, j)),
            pl.BlockSpec((1, wb.shape[1], bn), lambda i, j: (layer, 0, j)),
            pl.BlockSpec((1, wc.shape[1], bn), lambda i, j: (layer, 0, j)),
            gate(0), gate(1), gate(2),
        ],
        out_specs=pl.BlockSpec((bm, bn), lambda i, j: (i, j)),
        out_shape=jax.ShapeDtypeStruct((rows, d), BF16),
        compiler_params=_params(("arbitrary", "arbitrary"), blocks, 5 * _nbytes((bm, bn), F32)),
        name="branch_merge",
    )(ya, yb, yc, wa, wb, wc, z_main2, z_main2, z_main2)


def _out_proj_kernel(m_ref, w_ref, h_ref, g_ref, o_ref):
    o_ref[...] = h_ref[...] + g_ref[0] * _dot(m_ref[...], w_ref[0])


def _out_proj(m, w, layer, h2, mod3, gate_blk, rows, n):
    d = w.shape[2]
    bm = _pick(n, 512)
    bn = _pick(d, 1024)
    per_batch = n // bm
    gstep = d // bn
    blocks = _nbytes((bm, d), BF16) + _nbytes((d, bn), BF16) + 2 * _nbytes((bm, bn), F32)
    return pl.pallas_call(
        _out_proj_kernel,
        grid=(rows // bm, d // bn),
        in_specs=[
            pl.BlockSpec((bm, d), lambda i, j: (i, 0)),
            pl.BlockSpec((1, d, bn), lambda i, j: (layer, 0, j)),
            pl.BlockSpec((bm, bn), lambda i, j: (i, j)),
            pl.BlockSpec((1, 1, bn), lambda i, j: (i // per_batch, 0, gate_blk * gstep + j)),
        ],
        out_specs=pl.BlockSpec((bm, bn), lambda i, j: (i, j)),
        out_shape=jax.ShapeDtypeStruct((rows, d), F32),
        compiler_params=_params(("arbitrary", "arbitrary"), blocks, 2 * _nbytes((bm, bn), F32)),
        name="out_proj_residual",
    )(m, w, h2, mod3)


def _ffn_pre_kernel(x_ref, g_ref, sh_ref, sc_ref, wr_ref, br_ref, v_ref, idx_ref, wt_ref):
    d = x_ref.shape[2]
    v = _rms(x_ref[0], g_ref[...]) * (1.0 + sc_ref[0]) + sh_ref[0]
    v_ref[...] = _pack_bf16_pair(v[:, :d // 2], v[:, d // 2:]).reshape(v_ref.shape)
    logits = _dot(v.astype(BF16), wr_ref[...]) + br_ref[...]
    lane = lax.broadcasted_iota(jnp.int32, logits.shape, 1)
    lane_f = lane.astype(F32)
    vals, idxs = [], []
    for _ in range(TOP_K):
        m = logits.max(axis=-1, keepdims=True)
        pick = jnp.where(logits == m, lane_f, float(LANES)).min(axis=-1, keepdims=True)
        vals.append(m)
        idxs.append(pick.astype(jnp.int32))
        logits = jnp.where(lane_f == pick, -jnp.inf, logits)
    exps = [jnp.exp(val - vals[0]) for val in vals]
    den = exps[0]
    for e in exps[1:]:
        den = den + e
    idx_out = jnp.zeros(lane.shape, jnp.int32)
    wt_out = jnp.zeros(lane.shape, F32)
    for k in range(TOP_K):
        idx_out = jnp.where(lane == k, idxs[k], idx_out)
        wt_out = jnp.where(lane == k, exps[k] / den, wt_out)
    idx_ref[0] = idx_out
    wt_ref[0] = wt_out


def _ffn_pre(h, nb, gain, mod3, sh_blk, sc_blk, wr, br):
    _, n, d = h.shape
    bn = _pick(n, 256)
    blocks = (_nbytes((bn, d), F32) + _nbytes((bn, d // 2), U32) + 3 * _nbytes((1, d), F32)
              + _nbytes((d, LANES), BF16) + 2 * _nbytes((bn, LANES), F32))
    return pl.pallas_call(
        _ffn_pre_kernel,
        grid=(nb, n // bn),
        in_specs=[
            pl.BlockSpec((1, bn, d), lambda b, i: (b, i, 0)),
            pl.BlockSpec((1, d), lambda b, i: (0, 0)),
            pl.BlockSpec((1, 1, d), lambda b, i: (b, 0, sh_blk)),
            pl.BlockSpec((1, 1, d), lambda b, i: (b, 0, sc_blk)),
            pl.BlockSpec((d, LANES), lambda b, i: (0, 0)),
            pl.BlockSpec((1, LANES), lambda b, i: (0, 0)),
        ],
        out_specs=[
            pl.BlockSpec((bn, d // 2 // LANES, LANES), lambda b, i: (b * (n // bn) + i, 0, 0)),
            pl.BlockSpec((1, bn, LANES), lambda b, i: (b, i, 0)),
            pl.BlockSpec((1, bn, LANES), lambda b, i: (b, i, 0)),
        ],
        out_shape=[
            jax.ShapeDtypeStruct((nb * n, d // 2 // LANES, LANES), U32),
            jax.ShapeDtypeStruct((nb, n, LANES), jnp.int32),
            jax.ShapeDtypeStruct((nb, n, LANES), F32),
        ],
        compiler_params=_params(("arbitrary", "arbitrary"), blocks, 3 * _nbytes((bn, d), F32)),
        name="ffn_norm_router",
    )(h, gain.reshape(1, d), mod3, mod3, wr, br)


GATHER_UNROLL = 4
DMA_PRIORITIES = 2


def _start_row_gather(src_hbm, idx_ref, buf, sem, slot):
    rows = buf.shape[1]
    assert rows % DMA_PRIORITIES == 0

    def body(g, carry):
        for p in range(DMA_PRIORITIES):
            r = g * DMA_PRIORITIES + p
            pltpu.make_async_copy(src_hbm.at[idx_ref[0, 0, r]], buf.at[slot, r], sem.at[slot]).start(priority=p)
        return carry

    lax.fori_loop(0, rows // DMA_PRIORITIES, body, 0, unroll=GATHER_UNROLL)


def _wait_row_gather(src_hbm, buf, sem, slot):
    pltpu.make_async_copy(src_hbm.at[pl.ds(0, buf.shape[1])], buf.at[slot], sem.at[slot]).wait()


def _pipelined_row_gather(src_hbm, idx_cur_ref, idx_nxt_ref, buf, sem, valid_ref):
    t = pl.program_id(0)
    nt = pl.num_programs(0)
    slot = lax.rem(t, 2)
    nxt = jnp.minimum(t + 1, nt - 1)

    @pl.when(jnp.logical_and(t == 0, valid_ref[0] != 0))
    def _():
        _start_row_gather(src_hbm, idx_cur_ref, buf, sem, 0)

    @pl.when(jnp.logical_and(t + 1 < nt, valid_ref[nxt] != 0))
    def _():
        _start_row_gather(src_hbm, idx_nxt_ref, buf, sem, 1 - slot)

    @pl.when(valid_ref[t] != 0)
    def _():
        _wait_row_gather(src_hbm, buf, sem, slot)

    return slot


def _expert_kernel(te_ref, tv_ref, idx_cur_ref, idx_nxt_ref, x_hbm, wgu_ref, bgu_ref, wd_ref, bd_ref, o_ref, xbuf, sem):
    t = pl.program_id(0)
    bm = xbuf.shape[1]
    half = xbuf.shape[2] * xbuf.shape[3]
    ff = wd_ref.shape[1]
    slot = _pipelined_row_gather(x_hbm, idx_cur_ref, idx_nxt_ref, xbuf, sem, tv_ref)

    @pl.when(tv_ref[t] != 0)
    def _():
        lo, hi = _unpack_bf16_pair(xbuf[slot].reshape(bm, half))
        hgu = _dot(lo.astype(BF16), wgu_ref[0, :half]) + _dot(hi.astype(BF16), wgu_ref[0, half:]) + bgu_ref[0]
        gate = jnp.minimum(hgu[:, :ff], SWIGLU_LIMIT)
        lin = jnp.clip(hgu[:, ff:], -SWIGLU_LIMIT, SWIGLU_LIMIT)
        act = gate * jax.nn.sigmoid(SWIGLU_ALPHA * gate) * (lin + 1.0)
        y = _dot(act.astype(BF16), wd_ref[0]) + bd_ref[0]
        o_ref[...] = _pack_bf16_pair(y[:, :half], y[:, half:]).reshape(o_ref.shape)

    @pl.when(tv_ref[t] == 0)
    def _():
        o_ref[...] = jnp.zeros(o_ref.shape, o_ref.dtype)


def _experts(v_rows, row_token, tile_expert, tile_valid, wgu, bgu, wd, bd, e0, bm):
    _, s, _ = v_rows.shape
    rows = row_token.shape[0]
    e, d, ff2 = wgu.shape
    ff = ff2 // 2
    nt = rows // bm
    idx3 = row_token.reshape(nt, 1, bm)
    blocks = (_nbytes((bm, s, LANES), U32) + _nbytes((d, ff2), BF16) + _nbytes((ff, d), BF16)
              + _nbytes((1, ff2 + d), F32))
    temps = (2 * _nbytes((bm, s, LANES), U32) + 2 * _nbytes((bm, d), F32) + 2 * _nbytes((bm, d), BF16)
             + 3 * _nbytes((bm, ff2), F32))
    grid_spec = pltpu.PrefetchScalarGridSpec(
        num_scalar_prefetch=2,
        grid=(nt,),
        in_specs=[
            pl.BlockSpec((1, 1, bm), lambda t, te, tv: (t, 0, 0), memory_space=pltpu.SMEM),
            pl.BlockSpec((1, 1, bm), lambda t, te, tv: (jnp.minimum(t + 1, nt - 1), 0, 0), memory_space=pltpu.SMEM),
            pl.BlockSpec(memory_space=pl.ANY),
            pl.BlockSpec((1, d, ff2), lambda t, te, tv: (e0 + te[t], 0, 0)),
            pl.BlockSpec((1, 1, ff2), lambda t, te, tv: (e0 + te[t], 0, 0)),
            pl.BlockSpec((1, ff, d), lambda t, te, tv: (e0 + te[t], 0, 0)),
            pl.BlockSpec((1, 1, d), lambda t, te, tv: (e0 + te[t], 0, 0)),
        ],
        out_specs=pl.BlockSpec((bm, s, LANES), lambda t, te, tv: (t, 0, 0)),
        scratch_shapes=[pltpu.VMEM((2, bm, s, LANES), U32), pltpu.SemaphoreType.DMA((2,))],
    )
    return pl.pallas_call(
        _expert_kernel,
        grid_spec=grid_spec,
        out_shape=jax.ShapeDtypeStruct((rows, s, LANES), U32),
        compiler_params=_params(("arbitrary",), blocks, temps),
        name="experts",
    )(tile_expert, tile_valid, idx3, idx3, v_rows, wgu, bgu.reshape(e, 1, ff2), wd, bd.reshape(e, 1, d))


def _combine_kernel(tv_ref, idx_cur_ref, idx_nxt_ref, y_hbm, wt_ref, h_ref, g_ref, gf_ref, o_ref, ybuf, sem, *, final):
    bm = h_ref.shape[0]
    half = h_ref.shape[1] // 2
    slot = _pipelined_row_gather(y_hbm, idx_cur_ref, idx_nxt_ref, ybuf, sem, tv_ref)
    f_lo = None
    f_hi = None
    for k in range(TOP_K):
        lo, hi = _unpack_bf16_pair(ybuf[slot, k * bm:(k + 1) * bm].reshape(bm, half))
        w = wt_ref[:, k:k + 1]
        f_lo = lo * w if f_lo is None else f_lo + lo * w
        f_hi = hi * w if f_hi is None else f_hi + hi * w
    o_lo = h_ref[:, :half] + g_ref[0, :, :half] * f_lo
    o_hi = h_ref[:, half:] + g_ref[0, :, half:] * f_hi
    if final:
        ms = (jnp.sum(o_lo * o_lo, axis=-1, keepdims=True) + jnp.sum(o_hi * o_hi, axis=-1, keepdims=True)) / (2 * half)
        inv = lax.rsqrt(ms + NORM_EPS)
        o_lo = o_lo * inv * gf_ref[:, :half]
        o_hi = o_hi * inv * gf_ref[:, half:]
    o_ref[:, :half] = o_lo
    o_ref[:, half:] = o_hi


def _combine(y_rows, dest, wts2, h2, mod3, gate_blk, gain_final, rows, n, final):
    d = h2.shape[1]
    s = y_rows.shape[1]
    bm = _pick(n, 256)
    per_batch = n // bm
    nt = rows // bm
    idx3 = dest.reshape(nt, bm, TOP_K).transpose(0, 2, 1).reshape(nt, 1, TOP_K * bm)
    all_valid = jnp.ones((nt,), jnp.int32)
    blocks = _nbytes((bm, LANES), F32) + 2 * _nbytes((bm, d), F32) + 2 * _nbytes((1, d), F32)
    temps = 2 * _nbytes((TOP_K * bm, s, LANES), U32) + 6 * _nbytes((bm, d), F32)
    grid_spec = pltpu.PrefetchScalarGridSpec(
        num_scalar_prefetch=1,
        grid=(nt,),
        in_specs=[
            pl.BlockSpec((1, 1, TOP_K * bm), lambda i, tv: (i, 0, 0), memory_space=pltpu.SMEM),
            pl.BlockSpec((1, 1, TOP_K * bm), lambda i, tv: (jnp.minimum(i + 1, nt - 1), 0, 0),
                         memory_space=pltpu.SMEM),
            pl.BlockSpec(memory_space=pl.ANY),
            pl.BlockSpec((bm, LANES), lambda i, tv: (i, 0)),
            pl.BlockSpec((bm, d), lambda i, tv: (i, 0)),
            pl.BlockSpec((1, 1, d), lambda i, tv: (i // per_batch, 0, gate_blk)),
            pl.BlockSpec((1, d), lambda i, tv: (0, 0)),
        ],
        out_specs=pl.BlockSpec((bm, d), lambda i, tv: (i, 0)),
        scratch_shapes=[pltpu.VMEM((2, TOP_K * bm, s, LANES), U32), pltpu.SemaphoreType.DMA((2,))],
    )
    return pl.pallas_call(
        functools.partial(_combine_kernel, final=final),
        grid_spec=grid_spec,
        out_shape=jax.ShapeDtypeStruct((rows, d), F32),
        compiler_params=_params(("arbitrary",), blocks, temps),
        name="moe_combine",
    )(all_valid, idx3, idx3, y_rows, wts2, h2, mod3, gain_final.reshape(1, d))


def _rope_tables(n, rot_dim):
    t = jnp.arange(n, dtype=jnp.int32)
    row = (t // GRID_W).astype(F32)
    col = (t % GRID_W).astype(F32)
    half = rot_dim // 2
    inv = 1.0 / (ROPE_THETA ** (jnp.arange(0, half, 2, dtype=F32) / half))
    ang_r = row[:, None] * inv[None, :]
    ang_c = col[:, None] * inv[None, :]
    cos = jnp.concatenate([jnp.cos(ang_r), jnp.cos(ang_r), jnp.cos(ang_c), jnp.cos(ang_c)], axis=1)
    sin = jnp.concatenate([-jnp.sin(ang_r), jnp.sin(ang_r), -jnp.sin(ang_c), jnp.sin(ang_c)], axis=1)
    pad = LANES - rot_dim
    cos = jnp.pad(cos, ((0, 0), (0, pad)), constant_values=1.0)
    sin = jnp.pad(sin, ((0, 0), (0, pad)))
    return jnp.stack([cos, jnp.ones_like(cos)]), jnp.stack([sin, jnp.zeros_like(sin)])


def _dft_cos_sin(n):
    f = 64 if n % 64 == 0 and n > 64 else 1
    m = jnp.arange(n, dtype=jnp.int32)

    def table(rows):
        ang = ((rows[:, None] * m[None, :]) % n).astype(F32) * (2.0 * math.pi / n)
        return jnp.cos(ang), jnp.sin(ang)

    ca, sa = table(jnp.arange(n // f, dtype=jnp.int32) * f)
    if f == 1:
        return ca, sa
    cb, sb = table(jnp.arange(f, dtype=jnp.int32))
    ca, sa, cb, sb = ca[:, None, :], sa[:, None, :], cb[None, :, :], sb[None, :, :]
    return (ca * cb - sa * sb).reshape(n, n), (sa * cb + ca * sb).reshape(n, n)


def _deinterleave_kernel(w_ref, p_ref, o_ref):
    cw = p_ref.shape[0]
    ff = o_ref.shape[2] // 2
    for j in range(w_ref.shape[2] // cw):
        y = _dot(w_ref[0, :, j * cw:(j + 1) * cw].astype(BF16), p_ref[...]).astype(o_ref.dtype)
        o_ref[0, :, j * (cw // 2):(j + 1) * (cw // 2)] = y[:, :cw // 2]
        o_ref[0, :, ff + j * (cw // 2):ff + (j + 1) * (cw // 2)] = y[:, cw // 2:]


def _deinterleave_cast(w):
    e, d, ff2 = w.shape
    cw = min(512, ff2)
    assert ff2 % cw == 0 and (cw // 2) % LANES == 0
    src = jnp.arange(cw, dtype=jnp.int32)
    dst = jnp.where(src % 2 == 0, src // 2, cw // 2 + src // 2)
    perm = (dst[:, None] == jnp.arange(cw, dtype=jnp.int32)[None, :]).astype(BF16)
    bk = _pick(d, 1024)
    blocks = _nbytes((bk, ff2), F32) + _nbytes((cw, cw), BF16) + _nbytes((bk, ff2), BF16)
    return pl.pallas_call(
        _deinterleave_kernel,
        grid=(e, d // bk),
        in_specs=[
            pl.BlockSpec((1, bk, ff2), lambda i, j: (i, j, 0)),
            pl.BlockSpec((cw, cw), lambda i, j: (0, 0)),
        ],
        out_specs=pl.BlockSpec((1, bk, ff2), lambda i, j: (i, j, 0)),
        out_shape=jax.ShapeDtypeStruct((e, d, ff2), BF16),
        compiler_params=_params(("arbitrary", "arbitrary"), blocks, 3 * _nbytes((bk, ff2), BF16)),
        name="deinterleave_cast",
    )(w, perm)


def _fourier_matrices(n, n_sub):
    cc, sc = _dft_cos_sin(FOURIER_GROUP_W)
    w_chan = (jnp.concatenate([cc, sc], axis=1) * FOURIER_GROUP_W ** -0.5).astype(BF16)
    cn, sn = _dft_cos_sin(n)
    lat = jnp.concatenate([cn, -sn], axis=1) * n ** -0.5
    m = n // n_sub
    cm, sm = _dft_cos_sin(m)
    eye = jnp.eye(n_sub, dtype=F32)
    ctx = jnp.concatenate([jnp.kron(eye, cm), -jnp.kron(eye, sm)], axis=1) * m ** -0.5
    return w_chan, jnp.stack([lat, ctx]).astype(BF16)


def _stacked_weights(w_in, w_br_gqa, w_br_mla, w_br_fourier, w_out, w_gate_up, b_gate_up, w_down, b_down):
    depth, e = w_gate_up.shape[:2]
    kv_cols = 2 * GQA_KV_W + MLA_KV_LORA + MLA_ROPE_DIM
    w_kv = jnp.pad(w_in[:, :, :kv_cols], ((0, 0), (0, 0), (0, KV_W - kv_cols))).astype(BF16)
    w_main = w_in[:, :, kv_cols:].astype(BF16)
    wgu = _deinterleave_cast(w_gate_up.reshape((depth * e,) + w_gate_up.shape[2:]))
    bgu = jnp.concatenate([b_gate_up[..., 0::2], b_gate_up[..., 1::2]], axis=-1).reshape(depth * e, -1)
    wd = w_down.astype(BF16).reshape((depth * e,) + w_down.shape[2:])
    return dict(w_kv=w_kv, w_main=w_main, w_br_gqa=w_br_gqa.astype(BF16), w_br_mla=w_br_mla.astype(BF16),
                w_br_fourier=w_br_fourier.astype(BF16), w_out=w_out.astype(BF16), wgu=wgu, bgu=bgu, wd=wd,
                bd=b_down.reshape(depth * e, -1))


def _layer_weights(l, mla_w_uq, mla_w_ukv, w_router, b_router):
    uq = mla_w_uq[l].reshape(MLA_Q_LORA, MLA_HEADS, MLA_QK_DIM)
    uq_rope = jnp.pad(uq[:, :, MLA_NOPE_DIM:], ((0, 0), (0, 0), (0, LANES - MLA_ROPE_DIM)))
    w_uq = jnp.concatenate([uq[:, :, :MLA_NOPE_DIM].reshape(MLA_Q_LORA, -1), uq_rope.reshape(MLA_Q_LORA, -1)],
                           axis=1).astype(BF16)
    ukv = mla_w_ukv[l].reshape(MLA_KV_LORA, MLA_HEADS, MLA_NOPE_DIM + MLA_V_DIM)
    w_ukv = jnp.concatenate([ukv[:, :, :MLA_NOPE_DIM].reshape(MLA_KV_LORA, -1),
                             ukv[:, :, MLA_NOPE_DIM:].reshape(MLA_KV_LORA, -1)], axis=1).astype(BF16)
    e = w_router.shape[2]
    wr = jnp.pad(w_router[l], ((0, 0), (0, LANES - e))).astype(BF16)
    br = jnp.pad(b_router[l], (0, LANES - e), constant_values=NEG_BIG).reshape(1, LANES)
    return dict(w_uq=w_uq, w_ukv=w_ukv, wr=wr, br=br, n_experts=e)


def _route(idx, n_experts, bm):
    pairs = idx.shape[0] * TOP_K
    flat_e = idx.reshape(pairs)
    onehot = (flat_e[:, None] == jnp.arange(n_experts, dtype=jnp.int32)[None, :]).astype(jnp.int32)
    csum = jnp.cumsum(onehot, axis=0)
    counts = csum[-1]
    tiles_e = (counts + bm - 1) // bm
    tile_end = jnp.cumsum(tiles_e)
    tile_start = tile_end - tiles_e
    dest = jnp.sum(onehot * (csum - 1 + tile_start[None, :] * bm), axis=1)
    n_tiles = pairs // bm + n_experts
    rows = n_tiles * bm
    row_token = jnp.zeros((rows,), jnp.int32).at[dest].set(jnp.arange(pairs, dtype=jnp.int32) // TOP_K)
    tile_ids = jnp.arange(n_tiles, dtype=jnp.int32)
    tile_expert = jnp.minimum(jnp.sum((tile_ids[:, None] >= tile_end[None, :]).astype(jnp.int32), axis=1),
                              n_experts - 1)
    tile_valid = (tile_ids < tile_end[-1]).astype(jnp.int32)
    return row_token, dest, tile_expert, tile_valid


def _layer(h_all, mod3, l, sw, lw, gains, tabs, fmats, n_lat, ctx_len, last, norm_final):
    nb_all, n, d = h_all.shape
    nb = n_lat if last else nb_all
    rows = nb * n
    cg, sg, cm, sm = tabs
    w_chan, w_pos = fmats

    u = _norm_mod(h_all, gains["norm_mix"], mod3, 0, 1)
    u2 = u.reshape(nb_all * n, d)
    z_kv = _matmul(u2, sw["w_kv"], l, nb_all * n).reshape(nb_all, n, KV_W)
    z_main2 = _matmul(u2, sw["w_main"], l, rows)
    z_main = z_main2.reshape(nb, n, -1)
    ka, kvb, kr = _prep_kv(z_kv, gains["gqa_k_norm"], gains["mla_kv_norm"], lw["w_ukv"], (cg, sg, cm, sm), n_lat)
    qa, qb = _prep_q(z_main, nb, gains["gqa_q_norm"], gains["mla_q_norm"], lw["w_uq"], (cg, sg, cm, sm), n_lat)
    y_a = _gqa_attention(qa, ka, z_kv, nb, n_lat, ctx_len)
    y_b = _mla_attention(qb, kvb, kr, nb, n_lat, ctx_len)
    y_c = _fourier(z_main, nb, n_lat, w_chan, w_pos)
    m = _merge(y_a.reshape(rows, -1), y_b.reshape(rows, -1), y_c.reshape(rows, -1),
               sw["w_br_gqa"], sw["w_br_mla"], sw["w_br_fourier"], l, z_main2, rows, d)
    h2 = _out_proj(m, sw["w_out"], l, h_all.reshape(nb_all * n, d), mod3, 2, rows, n)

    v_packed, idx, wts = _ffn_pre(h2.reshape(nb, n, d), nb, gains["norm_ffn"], mod3, 3, 4, lw["wr"], lw["br"])
    n_experts = lw["n_experts"]
    bm = 256
    row_token, dest, tile_expert, tile_valid = _route(idx.reshape(rows, LANES)[:, :TOP_K], n_experts, bm)
    y_rows = _experts(v_packed, row_token, tile_expert, tile_valid, sw["wgu"], sw["bgu"], sw["wd"], sw["bd"],
                      l * n_experts, bm)
    out = _combine(y_rows, dest, wts.reshape(rows, LANES), h2, mod3, 5, norm_final, rows, n, last)
    return out.reshape(nb, n, d)


def kernel(x, c, ctx, c_ctx, w_ada, b_ada, norm_mix, w_in, gqa_q_norm, gqa_k_norm, mla_q_norm, mla_w_uq, mla_kv_norm, mla_w_ukv, w_br_gqa, w_br_mla, w_br_fourier, w_out, norm_ffn, w_router, b_router, w_gate_up, b_gate_up, w_down, b_down, norm_final):
    n_lat, n, d = x.shape
    ctx_len = ctx.shape[1]
    depth = w_ada.shape[0]
    assert n_lat * ctx_len == n, "context tokens must fill exactly one extra batch row"
    assert n % GRID_W == 0 and ctx_len % 8 == 0

    tabs_g = _rope_tables(n, HEAD_DIM)
    tabs_m = _rope_tables(n, MLA_ROPE_DIM)
    tabs = (tabs_g[0], tabs_g[1], tabs_m[0], tabs_m[1])
    fmats = _fourier_matrices(n, n_lat)

    mod_rows = 16
    c_all = jnp.zeros((mod_rows, d), F32).at[:n_lat].set(c).at[n_lat].set(c_ctx)
    mod = _ada_mod(c_all, w_ada, b_ada)

    sw = _stacked_weights(w_in, w_br_gqa, w_br_mla, w_br_fourier, w_out, w_gate_up, b_gate_up, w_down, b_down)
    h_all = jnp.concatenate([x, ctx.reshape(1, n, d)], axis=0)
    for l in range(depth):
        last = l == depth - 1
        lw = _layer_weights(l, mla_w_uq, mla_w_ukv, w_router, b_router)
        gains = dict(norm_mix=norm_mix[l], gqa_q_norm=gqa_q_norm[l], gqa_k_norm=gqa_k_norm[l],
                     mla_q_norm=mla_q_norm[l], mla_kv_norm=mla_kv_norm[l], norm_ffn=norm_ffn[l])
        mod3 = mod[l].reshape(mod_rows, 1, 6 * d)
        h_all = _layer(h_all, mod3, l, sw, lw, gains, tabs, fmats, n_lat, ctx_len, last, norm_final)
    return h_all
```

```python
import functools
import math

import jax
import jax.numpy as jnp
from jax import lax
from jax.experimental import pallas as pl
from jax.experimental.pallas import tpu as pltpu

F32 = jnp.float32
BF16 = jnp.bfloat16
U32 = jnp.uint32

GRID_W = 64
ROPE_THETA = 10000.0
NORM_EPS = 1e-6
HEAD_DIM = 128
GQA_HEADS = 16
GQA_KV_HEADS = 4
GQA_GROUP = GQA_HEADS // GQA_KV_HEADS
GQA_Q_W = GQA_HEADS * HEAD_DIM
GQA_KV_W = GQA_KV_HEADS * HEAD_DIM
GQA_SCALE = HEAD_DIM ** -0.5
MLA_HEADS = 8
MLA_Q_LORA = 1024
MLA_KV_LORA = 512
MLA_NOPE_DIM = 128
MLA_ROPE_DIM = 64
MLA_V_DIM = 128
MLA_QK_DIM = MLA_NOPE_DIM + MLA_ROPE_DIM
MLA_OUT_W = MLA_HEADS * MLA_V_DIM
MLA_SCALE = MLA_QK_DIM ** -0.5
FOURIER_GROUPS = 4
FOURIER_GROUP_W = 256
FOURIER_W = FOURIER_GROUPS * FOURIER_GROUP_W
TOP_K = 4
SWIGLU_ALPHA = 1.702
SWIGLU_LIMIT = 7.0
LOG2E = math.log2(math.e)
GQA_Q_SCALE = GQA_SCALE * LOG2E
MLA_Q_SCALE = MLA_SCALE * LOG2E

LANES = 128
V7X_VMEM_BYTES = 64 * 2 ** 20
VMEM_CAP = V7X_VMEM_BYTES - 6 * 2 ** 20
VMEM_FLOOR = 32 * 2 ** 20
VMEM_INTERNAL = 8 * 2 ** 20

KV_W = 2048
KV_OFF_K, KV_OFF_V, KV_OFF_CKV, KV_OFF_KR = 0, 512, 1024, 1536
MAIN_OFF_QA, MAIN_OFF_CQ, MAIN_OFF_F, MAIN_OFF_G = 0, 2048, 3072, 4096
NEG_BIG = -1e30


def _vmem(block_bytes, temp_bytes=0):
    return int(min(VMEM_CAP, max(VMEM_FLOOR, 2 * block_bytes + temp_bytes + VMEM_INTERNAL)))


def _params(sem, block_bytes, temp_bytes=0):
    return pltpu.CompilerParams(dimension_semantics=sem, vmem_limit_bytes=_vmem(block_bytes, temp_bytes))


def _nbytes(shape, dtype):
    return math.prod(shape) * jnp.dtype(dtype).itemsize


def _pick(n, target):
    if n <= target:
        return n
    for cand in range(target, 7, -1):
        if n % cand == 0 and cand % 8 == 0:
            return cand
    return n


def _rms(x, gain):
    return x * lax.rsqrt(jnp.mean(x * x, axis=-1, keepdims=True) + NORM_EPS) * gain


def _rope(x, cos, sin, blk):
    lane = lax.broadcasted_iota(jnp.int32, x.shape, x.ndim - 1)
    first = (lane % (2 * blk)) < blk
    swapped = jnp.where(first, pltpu.roll(x, LANES - blk, x.ndim - 1), pltpu.roll(x, blk, x.ndim - 1))
    return x * cos + swapped * sin


def _pack_bf16_pair(lo, hi):
    lo_bits = lax.bitcast_convert_type(lo.astype(BF16).astype(F32), U32) >> 16
    hi_bits = lax.bitcast_convert_type(hi.astype(BF16).astype(F32), U32) & jnp.uint32(0xFFFF0000)
    return lo_bits | hi_bits


def _unpack_bf16_pair(word):
    lo = lax.bitcast_convert_type(word << 16, F32)
    hi = lax.bitcast_convert_type(word & jnp.uint32(0xFFFF0000), F32)
    return lo, hi


def _dot(a, b):
    return jnp.dot(a, b, preferred_element_type=F32)


def _dot_nt(a, b):
    return lax.dot_general(a, b, (((1,), (1,)), ((), ())), preferred_element_type=F32)


def _ada_kernel(c_ref, w_ref, b_ref, o_ref):
    c = c_ref[...]
    act = (c * jax.nn.sigmoid(c)).astype(BF16)
    o_ref[0] = _dot(act, w_ref[0].astype(BF16)) + b_ref[0]


def _ada_mod(c_all, w_ada, b_ada):
    depth, d, n6 = w_ada.shape
    rows = c_all.shape[0]
    bn = _pick(n6, 512)
    blocks = _nbytes((rows, d), F32) + _nbytes((d, bn), F32) + _nbytes((rows, bn), F32) * 2
    return pl.pallas_call(
        _ada_kernel,
        grid=(depth, n6 // bn),
        in_specs=[
            pl.BlockSpec((rows, d), lambda l, j: (0, 0)),
            pl.BlockSpec((1, d, bn), lambda l, j: (l, 0, j)),
            pl.BlockSpec((1, 1, bn), lambda l, j: (l, 0, j)),
        ],
        out_specs=pl.BlockSpec((1, rows, bn), lambda l, j: (l, 0, j)),
        out_shape=jax.ShapeDtypeStruct((depth, rows, n6), F32),
        compiler_params=_params(("arbitrary", "arbitrary"), blocks, _nbytes((d, bn), BF16)),
        name="adaln_mod",
    )(c_all, w_ada, b_ada.reshape(depth, 1, n6))


def _norm_mod_kernel(*refs, n_lat):
    if n_lat is None:
        g_ref, sh_ref, sc_ref, x_ref, o_ref = refs
    else:
        g_ref, sh_ref, sc_ref, x_ref, c_ref, o_ref = refs

    def emit(src_ref):
        y = _rms(src_ref[0], g_ref[...])
        o_ref[0] = (y * (1.0 + sc_ref[0]) + sh_ref[0]).astype(o_ref.dtype)

    if n_lat is None:
        emit(x_ref)
    else:
        b = pl.program_id(0)
        pl.when(b < n_lat)(functools.partial(emit, x_ref))
        pl.when(b >= n_lat)(functools.partial(emit, c_ref))


def _norm_mod(h_parts, gain, mod3, sh_blk, sc_blk):
    n_lat, n, d = h_parts[0].shape
    bn = _pick(n, 256)
    nblk = n // bn
    blocks = 2 * _nbytes((bn, d), F32) + _nbytes((bn, d), BF16) + 3 * _nbytes((1, d), F32)
    if len(h_parts) == 1:
        nb = n_lat
        n_lat = None
        h_specs = [pl.BlockSpec((1, bn, d), lambda b, i: (b, i, 0))]
    else:
        nb = n_lat + h_parts[1].shape[0]
        h_specs = [
            pl.BlockSpec((1, bn, d), lambda b, i: (jnp.minimum(b, n_lat - 1), jnp.where(b < n_lat, i, nblk - 1), 0)),
            pl.BlockSpec((1, bn, d), lambda b, i: (jnp.maximum(b - n_lat, 0), jnp.where(b < n_lat, 0, i), 0)),
        ]
    return pl.pallas_call(
        functools.partial(_norm_mod_kernel, n_lat=n_lat),
        grid=(nb, nblk),
        in_specs=[
            pl.BlockSpec((1, d), lambda b, i: (0, 0)),
            pl.BlockSpec((1, 1, d), lambda b, i: (b, 0, sh_blk)),
            pl.BlockSpec((1, 1, d), lambda b, i: (b, 0, sc_blk)),
        ] + h_specs,
        out_specs=pl.BlockSpec((1, bn, d), lambda b, i: (b, i, 0)),
        out_shape=jax.ShapeDtypeStruct((nb, n, d), BF16),
        compiler_params=_params(("arbitrary", "arbitrary"), blocks, 2 * _nbytes((bn, d), F32)),
        name="norm_mod",
    )(gain.reshape(1, d), mod3, mod3, *h_parts)


def _mm_kernel(x_ref, w_ref, o_ref):
    o_ref[...] = _dot(x_ref[...], w_ref[0]).astype(o_ref.dtype)


def _matmul(x2, w, layer, rows, out_dtype=BF16):
    k = x2.shape[1]
    n = w.shape[2]
    bm = _pick(rows, 1024)
    bn = _pick(n, 1024)
    blocks = _nbytes((bm, k), BF16) + _nbytes((k, bn), BF16) + _nbytes((bm, bn), out_dtype)
    return pl.pallas_call(
        _mm_kernel,
        grid=(rows // bm, n // bn),
        in_specs=[
            pl.BlockSpec((bm, k), lambda i, j: (i, 0)),
            pl.BlockSpec((1, k, bn), lambda i, j: (layer, 0, j)),
        ],
        out_specs=pl.BlockSpec((bm, bn), lambda i, j: (i, j)),
        out_shape=jax.ShapeDtypeStruct((rows, n), out_dtype),
        compiler_params=_params(("arbitrary", "arbitrary"), blocks, _nbytes((bm, bn), F32)),
        name="matmul",
    )(x2, w)


def _prep_kv_kernel(z_ref, gk_ref, gkv_ref, wukv_ref, cg_ref, sg_ref, cm_ref, sm_ref, ka_ref, kvb_ref, kr_ref):
    gk = gk_ref[...]
    cg, sg = cg_ref[0], sg_ref[0]
    for h in range(GQA_KV_HEADS):
        sl = slice(KV_OFF_K + h * HEAD_DIM, KV_OFF_K + (h + 1) * HEAD_DIM)
        y = _rope(_rms(z_ref[0, :, sl].astype(F32), gk), cg, sg, HEAD_DIM // 4)
        ka_ref[0, :, h * HEAD_DIM:(h + 1) * HEAD_DIM] = y.astype(ka_ref.dtype)
    ckv = _rms(z_ref[0, :, KV_OFF_CKV:KV_OFF_CKV + MLA_KV_LORA].astype(F32), gkv_ref[...])
    kvb_ref[0] = _dot(ckv.astype(BF16), wukv_ref[...]).astype(kvb_ref.dtype)
    kr = z_ref[0, :, KV_OFF_KR:KV_OFF_KR + LANES].astype(F32)
    kr_ref[0] = _rope(kr, cm_ref[0], sm_ref[0], MLA_ROPE_DIM // 4).astype(kr_ref.dtype)


def _prep_kv(z_kv, gk, gkv, wukv, tabs, n_lat):
    nb, n, _ = z_kv.shape
    bn = _pick(n, 512)
    kvw = wukv.shape[1]
    tab = pl.BlockSpec((1, bn, LANES), lambda b, i: (b // n_lat, i, 0))
    blocks = (_nbytes((bn, KV_W), BF16) + _nbytes(wukv.shape, BF16) + 4 * _nbytes((bn, LANES), F32)
              + _nbytes((bn, GQA_KV_W + kvw + LANES), BF16))
    return pl.pallas_call(
        _prep_kv_kernel,
        grid=(nb, n // bn),
        in_specs=[
            pl.BlockSpec((1, bn, KV_W), lambda b, i: (b, i, 0)),
            pl.BlockSpec((1, HEAD_DIM), lambda b, i: (0, 0)),
            pl.BlockSpec((1, MLA_KV_LORA), lambda b, i: (0, 0)),
            pl.BlockSpec(wukv.shape, lambda b, i: (0, 0)),
            tab, tab, tab, tab,
        ],
        out_specs=[
            pl.BlockSpec((1, bn, GQA_KV_W), lambda b, i: (b, i, 0)),
            pl.BlockSpec((1, bn, kvw), lambda b, i: (b, i, 0)),
            pl.BlockSpec((1, bn, LANES), lambda b, i: (b, i, 0)),
        ],
        out_shape=[
            jax.ShapeDtypeStruct((nb, n, GQA_KV_W), BF16),
            jax.ShapeDtypeStruct((nb, n, kvw), BF16),
            jax.ShapeDtypeStruct((nb, n, LANES), BF16),
        ],
        compiler_params=_params(("arbitrary", "arbitrary"), blocks, 4 * _nbytes((bn, kvw), F32)),
        name="prep_kv",
    )(z_kv, gk.reshape(1, HEAD_DIM), gkv.reshape(1, MLA_KV_LORA), wukv, *tabs)


def _prep_q_kernel(zq_ref, zcq_ref, gq_ref, gcq_ref, wuq_ref, cg_ref, sg_ref, cm_ref, sm_ref, qa_ref, qb_ref):
    gq = gq_ref[...]
    cg, sg = cg_ref[0], sg_ref[0]
    for h in range(GQA_HEADS):
        sl = slice(h * HEAD_DIM, (h + 1) * HEAD_DIM)
        y = _rope(_rms(zq_ref[0, :, sl].astype(F32), gq), cg, sg, HEAD_DIM // 4)
        qa_ref[0, :, sl] = (y * GQA_Q_SCALE).astype(qa_ref.dtype)
    cq = _rms(zcq_ref[0].astype(F32), gcq_ref[...])
    q = _dot(cq.astype(BF16), wuq_ref[...])
    nope_w = MLA_HEADS * MLA_NOPE_DIM
    qb_ref[0, :, :nope_w] = (q[:, :nope_w] * MLA_Q_SCALE).astype(qb_ref.dtype)
    cm, sm = cm_ref[0], sm_ref[0]
    for h in range(MLA_HEADS):
        sl = slice(nope_w + h * LANES, nope_w + (h + 1) * LANES)
        qb_ref[0, :, sl] = (_rope(q[:, sl], cm, sm, MLA_ROPE_DIM // 4) * MLA_Q_SCALE).astype(qb_ref.dtype)


def _prep_q(z_main, nb, gq, gcq, wuq, tabs, n_lat):
    n = z_main.shape[1]
    bn = _pick(n, 256)
    qbw = wuq.shape[1]
    tab = pl.BlockSpec((1, bn, LANES), lambda b, i: (b // n_lat, i, 0))
    blocks = (_nbytes((bn, GQA_Q_W + MLA_Q_LORA), BF16) + _nbytes(wuq.shape, BF16)
              + 4 * _nbytes((bn, LANES), F32) + _nbytes((bn, GQA_Q_W + qbw), BF16))
    return pl.pallas_call(
        _prep_q_kernel,
        grid=(nb, n // bn),
        in_specs=[
            pl.BlockSpec((1, bn, GQA_Q_W), lambda b, i: (b, i, MAIN_OFF_QA // GQA_Q_W)),
            pl.BlockSpec((1, bn, MLA_Q_LORA), lambda b, i: (b, i, MAIN_OFF_CQ // MLA_Q_LORA)),
            pl.BlockSpec((1, HEAD_DIM), lambda b, i: (0, 0)),
            pl.BlockSpec((1, MLA_Q_LORA), lambda b, i: (0, 0)),
            pl.BlockSpec(wuq.shape, lambda b, i: (0, 0)),
            tab, tab, tab, tab,
        ],
        out_specs=[
            pl.BlockSpec((1, bn, GQA_Q_W), lambda b, i: (b, i, 0)),
            pl.BlockSpec((1, bn, qbw), lambda b, i: (b, i, 0)),
        ],
        out_shape=[
            jax.ShapeDtypeStruct((nb, n, GQA_Q_W), BF16),
            jax.ShapeDtypeStruct((nb, n, qbw), BF16),
        ],
        compiler_params=_params(("arbitrary", "arbitrary"), blocks, 4 * _nbytes((bn, qbw), F32)),
        name="prep_q",
    )(z_main, z_main, gq.reshape(1, HEAD_DIM), gcq.reshape(1, MLA_Q_LORA), wuq, *tabs)


def _online_softmax_pv(q, chunks):
    m = None
    acc = None
    for k, v, keep in chunks:
        v1 = jnp.concatenate([v, jnp.ones(v.shape, v.dtype)], axis=1)
        s = _dot_nt(q, k)
        if keep is not None:
            s = jnp.where(keep, s, NEG_BIG)
        mc = s.max(axis=-1, keepdims=True)
        if m is None:
            m = mc
            acc = _dot(jnp.exp2(s - m).astype(v.dtype), v1)
        else:
            m_new = jnp.maximum(m, mc)
            acc = acc * jnp.exp2(m - m_new) + _dot(jnp.exp2(s - m_new).astype(v.dtype), v1)
            m = m_new
    return acc[:, :HEAD_DIM] / acc[:, HEAD_DIM:HEAD_DIM + 1]


def _key_chunks(n, width):
    return [(c, min(width, n - c)) for c in range(0, n, width)]


GQA_KEY_CHUNK = 256
MLA_KEY_CHUNK = 512


def _gqa_kernel(q_ref, kc_ref, kl_ref, vc_ref, vl_ref, o_ref, *, n_lat, has_ctx_batch):
    bq = q_ref.shape[1]
    q = q_ref[0]
    qs = jnp.concatenate([q[:, g * HEAD_DIM:(g + 1) * HEAD_DIM] for g in range(GQA_GROUP)], axis=0)

    def finish(chunks):
        o = _online_softmax_pv(qs, chunks)
        for g in range(GQA_GROUP):
            o_ref[0, :, g * HEAD_DIM:(g + 1) * HEAD_DIM] = o[g * bq:(g + 1) * bq].astype(o_ref.dtype)

    def context():
        finish([(kc_ref[0], vc_ref[0], None)])

    def latent():
        finish([(kc_ref[0], vc_ref[0], None)]
               + [(kl_ref[0, c:c + w], vl_ref[0, c:c + w], None) for c, w in _key_chunks(kl_ref.shape[1], GQA_KEY_CHUNK)])

    if has_ctx_batch:
        b = pl.program_id(0)
        pl.when(b < n_lat)(latent)
        pl.when(b == n_lat)(context)
    else:
        latent()


def _mla_kernel(*refs, n_lat, ctx_len, has_ctx_batch):
    if has_ctx_batch:
        qn_ref, qr_ref, knc_ref, krc_ref, knl_ref, krl_ref, vc_ref, vl_ref, kns_ref, krs_ref, vs_ref, o_ref = refs
    else:
        qn_ref, qr_ref, knc_ref, krc_ref, knl_ref, krl_ref, vc_ref, vl_ref, o_ref = refs
    bq = qn_ref.shape[1]
    q = jnp.concatenate([qn_ref[0], qr_ref[0]], axis=1)

    def keys(kn_ref, kr_ref, c, w):
        return jnp.concatenate([kn_ref[0, c:c + w], kr_ref[0, c:c + w]], axis=1)

    def finish(chunks):
        o_ref[0] = _online_softmax_pv(q, chunks).astype(o_ref.dtype)

    def latent():
        finish([(keys(knc_ref, krc_ref, 0, ctx_len), vc_ref[0], None)]
               + [(keys(knl_ref, krl_ref, c, w), vl_ref[0, c:c + w], None)
                  for c, w in _key_chunks(knl_ref.shape[1], MLA_KEY_CHUNK)])

    def context():
        row = lax.broadcasted_iota(jnp.int32, (bq, bq), 0) // ctx_len
        col = lax.broadcasted_iota(jnp.int32, (bq, bq), 1) // ctx_len
        finish([(keys(kns_ref, krs_ref, 0, bq), vs_ref[0], row == col if bq > ctx_len else None)])

    if has_ctx_batch:
        b = pl.program_id(0)
        pl.when(b < n_lat)(latent)
        pl.when(b == n_lat)(context)
    else:
        latent()


def _ctx_idx(b, i, n_lat):
    return jnp.where(b < n_lat, b, i)


def _lat_idx(b, n_lat):
    return jnp.where(b < n_lat, b, 0)


def _gqa_attention(qa, ka, z_kv, nb, n_lat, ctx_len):
    n = qa.shape[1]
    bq = ctx_len
    qw = GQA_GROUP * HEAD_DIM
    v_blk = KV_OFF_V // HEAD_DIM
    kern = functools.partial(_gqa_kernel, n_lat=n_lat, has_ctx_batch=nb > n_lat)
    blocks = (2 * _nbytes((bq, qw), BF16) + 2 * _nbytes((ctx_len + n, HEAD_DIM), BF16))
    temps = 6 * _nbytes((GQA_GROUP * bq, max(ctx_len, GQA_KEY_CHUNK)), F32)
    return pl.pallas_call(
        kern,
        grid=(nb, GQA_KV_HEADS, n // bq),
        in_specs=[
            pl.BlockSpec((1, bq, qw), lambda b, h, i: (b, i, h)),
            pl.BlockSpec((1, ctx_len, HEAD_DIM), lambda b, h, i: (n_lat, _ctx_idx(b, i, n_lat), h)),
            pl.BlockSpec((1, n, HEAD_DIM), lambda b, h, i: (_lat_idx(b, n_lat), 0, h)),
            pl.BlockSpec((1, ctx_len, HEAD_DIM), lambda b, h, i: (n_lat, _ctx_idx(b, i, n_lat), v_blk + h)),
            pl.BlockSpec((1, n, HEAD_DIM), lambda b, h, i: (_lat_idx(b, n_lat), 0, v_blk + h)),
        ],
        out_specs=pl.BlockSpec((1, bq, qw), lambda b, h, i: (b, i, h)),
        out_shape=jax.ShapeDtypeStruct((nb, n, GQA_Q_W), BF16),
        compiler_params=_params(("arbitrary", "arbitrary", "arbitrary"), blocks, temps),
        name="gqa_attention",
    )(qa, ka, ka, z_kv, z_kv)


def _mla_attention(qb, kvb, kr, nb, n_lat, ctx_len):
    n = qb.shape[1]
    bq = ctx_len * max(1, min(n, 1024) // ctx_len)
    assert n % bq == 0
    has_ctx_batch = nb > n_lat
    kern = functools.partial(_mla_kernel, n_lat=n_lat, ctx_len=ctx_len, has_ctx_batch=has_ctx_batch)
    blocks = 6 * _nbytes((bq, LANES), BF16) + 3 * _nbytes((ctx_len + n, LANES), BF16)
    temps = 6 * _nbytes((bq, max(bq, MLA_KEY_CHUNK)), F32) + 4 * _nbytes((n, 2 * LANES), BF16)
    ctx_spec = lambda col: pl.BlockSpec((1, ctx_len, LANES), lambda b, h, i: (n_lat, _lat_idx(b, n_lat), col(h)))
    lat_spec = lambda col: pl.BlockSpec((1, n, LANES), lambda b, h, i: (_lat_idx(b, n_lat), 0, col(h)))
    self_spec = lambda col: pl.BlockSpec((1, bq, LANES), lambda b, h, i: (n_lat, jnp.where(b < n_lat, 0, i), col(h)))
    in_specs = [
        pl.BlockSpec((1, bq, LANES), lambda b, h, i: (b, i, h)),
        pl.BlockSpec((1, bq, LANES), lambda b, h, i: (b, i, MLA_HEADS + h)),
        ctx_spec(lambda h: h), ctx_spec(lambda h: 0),
        lat_spec(lambda h: h), lat_spec(lambda h: 0),
        ctx_spec(lambda h: MLA_HEADS + h), lat_spec(lambda h: MLA_HEADS + h),
    ]
    operands = [qb, qb, kvb, kr, kvb, kr, kvb, kvb]
    if has_ctx_batch:
        in_specs += [self_spec(lambda h: h), self_spec(lambda h: 0), self_spec(lambda h: MLA_HEADS + h)]
        operands += [kvb, kr, kvb]
    return pl.pallas_call(
        kern,
        grid=(nb, MLA_HEADS, n // bq),
        in_specs=in_specs,
        out_specs=pl.BlockSpec((1, bq, LANES), lambda b, h, i: (b, i, h)),
        out_shape=jax.ShapeDtypeStruct((nb, n, MLA_OUT_W), BF16),
        compiler_params=_params(("arbitrary", "arbitrary", "arbitrary"), blocks, temps),
        name="mla_attention",
    )(*operands)


def _fourier_kernel(x_ref, wc_ref, wp_ref, o_ref):
    gw = x_ref.shape[2]
    xcs = _dot(x_ref[0], wc_ref[...]).astype(BF16)
    stacked = jnp.concatenate([xcs[:, :gw], xcs[:, gw:]], axis=0)
    o_ref[0] = _dot(wp_ref[0], stacked).astype(o_ref.dtype)


def _fourier(z_main, nb, n_lat, w_chan, w_pos):
    n = z_main.shape[1]
    gw = FOURIER_GROUP_W
    blocks = 2 * _nbytes((n, gw), BF16) + _nbytes((gw, 2 * gw), BF16) + _nbytes((n, 2 * n), BF16)
    temps = _nbytes((n, 2 * gw), F32) + 2 * _nbytes((2 * n, gw), BF16) + _nbytes((n, gw), F32)
    return pl.pallas_call(
        _fourier_kernel,
        grid=(nb, FOURIER_GROUPS),
        in_specs=[
            pl.BlockSpec((1, n, gw), lambda b, g: (b, 0, MAIN_OFF_F // gw + g)),
            pl.BlockSpec((gw, 2 * gw), lambda b, g: (0, 0)),
            pl.BlockSpec((1, n, 2 * n), lambda b, g: (b // n_lat, 0, 0)),
        ],
        out_specs=pl.BlockSpec((1, n, gw), lambda b, g: (b, 0, g)),
        out_shape=jax.ShapeDtypeStruct((nb, n, FOURIER_W), BF16),
        compiler_params=_params(("arbitrary", "arbitrary"), blocks, temps),
        name="fourier_mix",
    )(z_main, w_chan, w_pos)


def _merge_kernel(ya_ref, yb_ref, yc_ref, wa_ref, wb_ref, wc_ref, ga_ref, gb_ref, gc_ref, o_ref):
    m = jax.nn.sigmoid(ga_ref[...].astype(F32)) * _dot(ya_ref[...], wa_ref[0])
    m = m + jax.nn.sigmoid(gb_ref[...].astype(F32)) * _dot(yb_ref[...], wb_ref[0])
    m = m + jax.nn.sigmoid(gc_ref[...].astype(F32)) * _dot(yc_ref[...], wc_ref[0])
    o_ref[...] = m.astype(o_ref.dtype)


def _merge(ya, yb, yc, wa, wb, wc, layer, z_main2, rows, d):
    bm = _pick(rows, 512)
    bn = _pick(d, 1024)
    g0 = MAIN_OFF_G // bn
    gstep = d // bn
    kin = ya.shape[1] + yb.shape[1] + yc.shape[1]
    blocks = _nbytes((bm, kin), BF16) + _nbytes((kin, bn), BF16) + 4 * _nbytes((bm, bn), BF16)
    gate = lambda k: pl.BlockSpec((bm, bn), lambda i, j: (i, g0 + k * gstep + j))
    return pl.pallas_call(
        _merge_kernel,
        grid=(rows // bm, d // bn),
        in_specs=[
            pl.BlockSpec((bm, ya.shape[1]), lambda i, j: (i, 0)),
            pl.BlockSpec((bm, yb.shape[1]), lambda i, j: (i, 0)),
            pl.BlockSpec((bm, yc.shape[1]), lambda i, j: (i, 0)),
            pl.BlockSpec((1, wa.shape[1], bn), lambda i, j: (layer, 0, j)),
            pl.BlockSpec((1, wb.shape[1], bn), lambda i, j: (layer, 0, j)),
            pl.BlockSpec((1, wc.shape[1], bn), lambda i, j: (layer, 0, j)),
            gate(0), gate(1), gate(2),
        ],
        out_specs=pl.BlockSpec((bm, bn), lambda i, j: (i, j)),
        out_shape=jax.ShapeDtypeStruct((rows, d), BF16),
        compiler_params=_params(("arbitrary", "arbitrary"), blocks, 5 * _nbytes((bm, bn), F32)),
        name="branch_merge",
    )(ya, yb, yc, wa, wb, wc, z_main2, z_main2, z_main2)


def _out_proj_kernel(*refs, lat_blocks):
    if lat_blocks is None:
        m_ref, w_ref, g_ref, h_ref, o_ref = refs
    else:
        m_ref, w_ref, g_ref, h_ref, hc_ref, o_ref = refs
    upd = g_ref[0] * _dot(m_ref[...], w_ref[0])
    if lat_blocks is None:
        o_ref[...] = h_ref[...] + upd
    else:
        i = pl.program_id(0)

        @pl.when(i < lat_blocks)
        def _():
            o_ref[...] = h_ref[...] + upd

        @pl.when(i >= lat_blocks)
        def _():
            o_ref[...] = hc_ref[...] + upd


def _out_proj(m, w, layer, h_parts, mod3, gate_blk, rows, n):
    d = w.shape[2]
    bm = _pick(n, 512)
    bn = _pick(d, 1024)
    per_batch = n // bm
    gstep = d // bn
    blocks = _nbytes((bm, d), BF16) + _nbytes((d, bn), BF16) + 3 * _nbytes((bm, bn), F32)
    if len(h_parts) == 1:
        lat_blocks = None
        h_specs = [pl.BlockSpec((bm, bn), lambda i, j: (i, j))]
    else:
        lat_blocks = h_parts[0].shape[0] // bm
        h_specs = [
            pl.BlockSpec((bm, bn), lambda i, j: (jnp.minimum(i, lat_blocks - 1),
                                                 jnp.where(i < lat_blocks, j, gstep - 1))),
            pl.BlockSpec((bm, bn), lambda i, j: (jnp.maximum(i - lat_blocks, 0),
                                                 jnp.where(i < lat_blocks, 0, j))),
        ]
    return pl.pallas_call(
        functools.partial(_out_proj_kernel, lat_blocks=lat_blocks),
        grid=(rows // bm, d // bn),
        in_specs=[
            pl.BlockSpec((bm, d), lambda i, j: (i, 0)),
            pl.BlockSpec((1, d, bn), lambda i, j: (layer, 0, j)),
            pl.BlockSpec((1, 1, bn), lambda i, j: (i // per_batch, 0, gate_blk * gstep + j)),
        ] + h_specs,
        out_specs=pl.BlockSpec((bm, bn), lambda i, j: (i, j)),
        out_shape=jax.ShapeDtypeStruct((rows, d), F32),
        compiler_params=_params(("arbitrary", "arbitrary"), blocks, 2 * _nbytes((bm, bn), F32)),
        name="out_proj_residual",
    )(m, w, mod3, *h_parts)


def _ffn_pre_kernel(x_ref, g_ref, sh_ref, sc_ref, wr_ref, br_ref, v_ref, idx_ref, wt_ref):
    d = x_ref.shape[2]
    v = _rms(x_ref[0], g_ref[...]) * (1.0 + sc_ref[0]) + sh_ref[0]
    v_ref[...] = _pack_bf16_pair(v[:, :d // 2], v[:, d // 2:]).reshape(v_ref.shape)
    logits = _dot(v.astype(BF16), wr_ref[...]) + br_ref[...]
    lane = lax.broadcasted_iota(jnp.int32, logits.shape, 1)
    lane_f = lane.astype(F32)
    vals, idxs = [], []
    for _ in range(TOP_K):
        m = logits.max(axis=-1, keepdims=True)
        pick = jnp.where(logits == m, lane_f, float(LANES)).min(axis=-1, keepdims=True)
        vals.append(m)
        idxs.append(pick.astype(jnp.int32))
        logits = jnp.where(lane_f == pick, -jnp.inf, logits)
    exps = [jnp.exp(val - vals[0]) for val in vals]
    den = exps[0]
    for e in exps[1:]:
        den = den + e
    idx_out = jnp.zeros(lane.shape, jnp.int32)
    wt_out = jnp.zeros(lane.shape, F32)
    for k in range(TOP_K):
        idx_out = jnp.where(lane == k, idxs[k], idx_out)
        wt_out = jnp.where(lane == k, exps[k] / den, wt_out)
    idx_ref[0] = idx_out
    wt_ref[0] = wt_out


def _ffn_pre(h, nb, gain, mod3, sh_blk, sc_blk, wr, br):
    _, n, d = h.shape
    bn = _pick(n, 256)
    blocks = (_nbytes((bn, d), F32) + _nbytes((bn, d // 2), U32) + 3 * _nbytes((1, d), F32)
              + _nbytes((d, LANES), BF16) + 2 * _nbytes((bn, LANES), F32))
    return pl.pallas_call(
        _ffn_pre_kernel,
        grid=(nb, n // bn),
        in_specs=[
            pl.BlockSpec((1, bn, d), lambda b, i: (b, i, 0)),
            pl.BlockSpec((1, d), lambda b, i: (0, 0)),
            pl.BlockSpec((1, 1, d), lambda b, i: (b, 0, sh_blk)),
            pl.BlockSpec((1, 1, d), lambda b, i: (b, 0, sc_blk)),
            pl.BlockSpec((d, LANES), lambda b, i: (0, 0)),
            pl.BlockSpec((1, LANES), lambda b, i: (0, 0)),
        ],
        out_specs=[
            pl.BlockSpec((bn, d // 2 // LANES, LANES), lambda b, i: (b * (n // bn) + i, 0, 0)),
            pl.BlockSpec((1, bn, LANES), lambda b, i: (b, i, 0)),
            pl.BlockSpec((1, bn, LANES), lambda b, i: (b, i, 0)),
        ],
        out_shape=[
            jax.ShapeDtypeStruct((nb * n, d // 2 // LANES, LANES), U32),
            jax.ShapeDtypeStruct((nb, n, LANES), jnp.int32),
            jax.ShapeDtypeStruct((nb, n, LANES), F32),
        ],
        compiler_params=_params(("arbitrary", "arbitrary"), blocks, 3 * _nbytes((bn, d), F32)),
        name="ffn_norm_router",
    )(h, gain.reshape(1, d), mod3, mod3, wr, br)


GATHER_UNROLL = 4
DMA_PRIORITIES = 2


def _start_row_gather(src_hbm, idx_ref, buf, sem, slot):
    rows = buf.shape[1]
    assert rows % DMA_PRIORITIES == 0

    def body(g, carry):
        for p in range(DMA_PRIORITIES):
            r = g * DMA_PRIORITIES + p
            pltpu.make_async_copy(src_hbm.at[idx_ref[0, 0, r]], buf.at[slot, r], sem.at[slot]).start(priority=p)
        return carry

    lax.fori_loop(0, rows // DMA_PRIORITIES, body, 0, unroll=GATHER_UNROLL)


def _wait_row_gather(src_hbm, buf, sem, slot):
    pltpu.make_async_copy(src_hbm.at[pl.ds(0, buf.shape[1])], buf.at[slot], sem.at[slot]).wait()


def _pipelined_row_gather(src_hbm, idx_cur_ref, idx_nxt_ref, buf, sem, valid_ref):
    t = pl.program_id(0)
    nt = pl.num_programs(0)
    slot = lax.rem(t, 2)
    nxt = jnp.minimum(t + 1, nt - 1)

    @pl.when(jnp.logical_and(t == 0, valid_ref[0] != 0))
    def _():
        _start_row_gather(src_hbm, idx_cur_ref, buf, sem, 0)

    @pl.when(jnp.logical_and(t + 1 < nt, valid_ref[nxt] != 0))
    def _():
        _start_row_gather(src_hbm, idx_nxt_ref, buf, sem, 1 - slot)

    @pl.when(valid_ref[t] != 0)
    def _():
        _wait_row_gather(src_hbm, buf, sem, slot)

    return slot


def _expert_kernel(te_ref, tv_ref, idx_cur_ref, idx_nxt_ref, x_hbm, wgu_ref, bgu_ref, wd_ref, bd_ref, o_ref, xbuf, sem):
    t = pl.program_id(0)
    bm = xbuf.shape[1]
    half = xbuf.shape[2] * xbuf.shape[3]
    ff = wd_ref.shape[1]
    slot = _pipelined_row_gather(x_hbm, idx_cur_ref, idx_nxt_ref, xbuf, sem, tv_ref)

    @pl.when(tv_ref[t] != 0)
    def _():
        lo, hi = _unpack_bf16_pair(xbuf[slot].reshape(bm, half))
        hgu = _dot(lo.astype(BF16), wgu_ref[0, :half]) + _dot(hi.astype(BF16), wgu_ref[0, half:]) + bgu_ref[0]
        gate = jnp.minimum(hgu[:, :ff], SWIGLU_LIMIT)
        lin = jnp.clip(hgu[:, ff:], -SWIGLU_LIMIT, SWIGLU_LIMIT)
        act = gate * jax.nn.sigmoid(SWIGLU_ALPHA * gate) * (lin + 1.0)
        y = _dot(act.astype(BF16), wd_ref[0]) + bd_ref[0]
        o_ref[...] = _pack_bf16_pair(y[:, :half], y[:, half:]).reshape(o_ref.shape)

    @pl.when(tv_ref[t] == 0)
    def _():
        o_ref[...] = jnp.zeros(o_ref.shape, o_ref.dtype)


def _experts(v_rows, row_token, tile_expert, tile_valid, wgu, bgu, wd, bd, e0, bm):
    _, s, _ = v_rows.shape
    rows = row_token.shape[0]
    e, d, ff2 = wgu.shape
    ff = ff2 // 2
    nt = rows // bm
    idx3 = row_token.reshape(nt, 1, bm)
    blocks = (_nbytes((bm, s, LANES), U32) + _nbytes((d, ff2), BF16) + _nbytes((ff, d), BF16)
              + _nbytes((1, ff2 + d), F32))
    temps = (2 * _nbytes((bm, s, LANES), U32) + 2 * _nbytes((bm, d), F32) + 2 * _nbytes((bm, d), BF16)
             + 3 * _nbytes((bm, ff2), F32))
    grid_spec = pltpu.PrefetchScalarGridSpec(
        num_scalar_prefetch=2,
        grid=(nt,),
        in_specs=[
            pl.BlockSpec((1, 1, bm), lambda t, te, tv: (t, 0, 0), memory_space=pltpu.SMEM),
            pl.BlockSpec((1, 1, bm), lambda t, te, tv: (jnp.minimum(t + 1, nt - 1), 0, 0), memory_space=pltpu.SMEM),
            pl.BlockSpec(memory_space=pl.ANY),
            pl.BlockSpec((1, d, ff2), lambda t, te, tv: (e0 + te[t], 0, 0)),
            pl.BlockSpec((1, 1, ff2), lambda t, te, tv: (e0 + te[t], 0, 0)),
            pl.BlockSpec((1, ff, d), lambda t, te, tv: (e0 + te[t], 0, 0)),
            pl.BlockSpec((1, 1, d), lambda t, te, tv: (e0 + te[t], 0, 0)),
        ],
        out_specs=pl.BlockSpec((bm, s, LANES), lambda t, te, tv: (t, 0, 0)),
        scratch_shapes=[pltpu.VMEM((2, bm, s, LANES), U32), pltpu.SemaphoreType.DMA((2,))],
    )
    return pl.pallas_call(
        _expert_kernel,
        grid_spec=grid_spec,
        out_shape=jax.ShapeDtypeStruct((rows, s, LANES), U32),
        compiler_params=_params(("arbitrary",), blocks, temps),
        name="experts",
    )(tile_expert, tile_valid, idx3, idx3, v_rows, wgu, bgu.reshape(e, 1, ff2), wd, bd.reshape(e, 1, d))


def _combine_kernel(tv_ref, idx_cur_ref, idx_nxt_ref, y_hbm, wt_ref, h_ref, g_ref, gf_ref, sh_ref, sc_ref, *rest, mode):
    if mode == "next":
        o_ref, u_ref, ybuf, sem = rest
    else:
        o_ref, ybuf, sem = rest
    bm = h_ref.shape[0]
    half = h_ref.shape[1] // 2
    slot = _pipelined_row_gather(y_hbm, idx_cur_ref, idx_nxt_ref, ybuf, sem, tv_ref)
    f_lo = None
    f_hi = None
    for k in range(TOP_K):
        lo, hi = _unpack_bf16_pair(ybuf[slot, k * bm:(k + 1) * bm].reshape(bm, half))
        w = wt_ref[:, k:k + 1]
        f_lo = lo * w if f_lo is None else f_lo + lo * w
        f_hi = hi * w if f_hi is None else f_hi + hi * w
    o_lo = h_ref[:, :half] + g_ref[0, :, :half] * f_lo
    o_hi = h_ref[:, half:] + g_ref[0, :, half:] * f_hi
    if mode != "plain":
        ms = (jnp.sum(o_lo * o_lo, axis=-1, keepdims=True) + jnp.sum(o_hi * o_hi, axis=-1, keepdims=True)) / (2 * half)
        inv = lax.rsqrt(ms + NORM_EPS)
        n_lo = o_lo * inv * gf_ref[:, :half]
        n_hi = o_hi * inv * gf_ref[:, half:]
        if mode == "final":
            o_lo, o_hi = n_lo, n_hi
        else:
            u_ref[:, :half] = (n_lo * (1.0 + sc_ref[0, :, :half]) + sh_ref[0, :, :half]).astype(u_ref.dtype)
            u_ref[:, half:] = (n_hi * (1.0 + sc_ref[0, :, half:]) + sh_ref[0, :, half:]).astype(u_ref.dtype)
    o_ref[:, :half] = o_lo
    o_ref[:, half:] = o_hi


def _combine(y_rows, dest, wts2, h2, mod3, gate_blk, rows, n, mode, gain, mod3_norm, sh_blk, sc_blk):
    d = h2.shape[1]
    s = y_rows.shape[1]
    bm = _pick(n, 256)
    per_batch = n // bm
    nt = rows // bm
    idx3 = dest.reshape(nt, bm, TOP_K).transpose(0, 2, 1).reshape(nt, 1, TOP_K * bm)
    all_valid = jnp.ones((nt,), jnp.int32)
    row_block = pl.BlockSpec((bm, d), lambda i, tv: (i, 0))
    blocks = _nbytes((bm, LANES), F32) + 3 * _nbytes((bm, d), F32) + 4 * _nbytes((1, d), F32)
    temps = 2 * _nbytes((TOP_K * bm, s, LANES), U32) + 6 * _nbytes((bm, d), F32)
    grid_spec = pltpu.PrefetchScalarGridSpec(
        num_scalar_prefetch=1,
        grid=(nt,),
        in_specs=[
            pl.BlockSpec((1, 1, TOP_K * bm), lambda i, tv: (i, 0, 0), memory_space=pltpu.SMEM),
            pl.BlockSpec((1, 1, TOP_K * bm), lambda i, tv: (jnp.minimum(i + 1, nt - 1), 0, 0),
                         memory_space=pltpu.SMEM),
            pl.BlockSpec(memory_space=pl.ANY),
            pl.BlockSpec((bm, LANES), lambda i, tv: (i, 0)),
            pl.BlockSpec((bm, d), lambda i, tv: (i, 0)),
            pl.BlockSpec((1, 1, d), lambda i, tv: (i // per_batch, 0, gate_blk)),
            pl.BlockSpec((1, d), lambda i, tv: (0, 0)),
            pl.BlockSpec((1, 1, d), lambda i, tv: (i // per_batch, 0, sh_blk)),
            pl.BlockSpec((1, 1, d), lambda i, tv: (i // per_batch, 0, sc_blk)),
        ],
        out_specs=[row_block, row_block] if mode == "next" else row_block,
        scratch_shapes=[pltpu.VMEM((2, TOP_K * bm, s, LANES), U32), pltpu.SemaphoreType.DMA((2,))],
    )
    out_shape = jax.ShapeDtypeStruct((rows, d), F32)
    return pl.pallas_call(
        functools.partial(_combine_kernel, mode=mode),
        grid_spec=grid_spec,
        out_shape=[out_shape, jax.ShapeDtypeStruct((rows, d), BF16)] if mode == "next" else out_shape,
        compiler_params=_params(("arbitrary",), blocks, temps),
        name="moe_combine",
    )(all_valid, idx3, idx3, y_rows, wts2, h2, mod3, gain.reshape(1, d), mod3_norm, mod3_norm)


def _rope_tables(n, rot_dim):
    t = jnp.arange(n, dtype=jnp.int32)
    row = (t // GRID_W).astype(F32)
    col = (t % GRID_W).astype(F32)
    half = rot_dim // 2
    inv = 1.0 / (ROPE_THETA ** (jnp.arange(0, half, 2, dtype=F32) / half))
    ang_r = row[:, None] * inv[None, :]
    ang_c = col[:, None] * inv[None, :]
    cos = jnp.concatenate([jnp.cos(ang_r), jnp.cos(ang_r), jnp.cos(ang_c), jnp.cos(ang_c)], axis=1)
    sin = jnp.concatenate([-jnp.sin(ang_r), jnp.sin(ang_r), -jnp.sin(ang_c), jnp.sin(ang_c)], axis=1)
    pad = LANES - rot_dim
    cos = jnp.pad(cos, ((0, 0), (0, pad)), constant_values=1.0)
    sin = jnp.pad(sin, ((0, 0), (0, pad)))
    return jnp.stack([cos, jnp.ones_like(cos)]), jnp.stack([sin, jnp.zeros_like(sin)])


def _dft_cos_sin(n):
    f = 64 if n % 64 == 0 and n > 64 else 1
    m = jnp.arange(n, dtype=jnp.int32)

    def table(rows):
        ang = ((rows[:, None] * m[None, :]) % n).astype(F32) * (2.0 * math.pi / n)
        return jnp.cos(ang), jnp.sin(ang)

    ca, sa = table(jnp.arange(n // f, dtype=jnp.int32) * f)
    if f == 1:
        return ca, sa
    cb, sb = table(jnp.arange(f, dtype=jnp.int32))
    ca, sa, cb, sb = ca[:, None, :], sa[:, None, :], cb[None, :, :], sb[None, :, :]
    return (ca * cb - sa * sb).reshape(n, n), (sa * cb + ca * sb).reshape(n, n)


def _deinterleave_kernel(w_ref, p_ref, o_ref):
    cw = p_ref.shape[0]
    ff = o_ref.shape[2] // 2
    for j in range(w_ref.shape[2] // cw):
        y = _dot(w_ref[0, :, j * cw:(j + 1) * cw].astype(BF16), p_ref[...]).astype(o_ref.dtype)
        o_ref[0, :, j * (cw // 2):(j + 1) * (cw // 2)] = y[:, :cw // 2]
        o_ref[0, :, ff + j * (cw // 2):ff + (j + 1) * (cw // 2)] = y[:, cw // 2:]


def _deinterleave_cast(w):
    e, d, ff2 = w.shape
    cw = min(512, ff2)
    assert ff2 % cw == 0 and (cw // 2) % LANES == 0
    src = jnp.arange(cw, dtype=jnp.int32)
    dst = jnp.where(src % 2 == 0, src // 2, cw // 2 + src // 2)
    perm = (dst[:, None] == jnp.arange(cw, dtype=jnp.int32)[None, :]).astype(BF16)
    bk = _pick(d, 1024)
    blocks = _nbytes((bk, ff2), F32) + _nbytes((cw, cw), BF16) + _nbytes((bk, ff2), BF16)
    return pl.pallas_call(
        _deinterleave_kernel,
        grid=(e, d // bk),
        in_specs=[
            pl.BlockSpec((1, bk, ff2), lambda i, j: (i, j, 0)),
            pl.BlockSpec((cw, cw), lambda i, j: (0, 0)),
        ],
        out_specs=pl.BlockSpec((1, bk, ff2), lambda i, j: (i, j, 0)),
        out_shape=jax.ShapeDtypeStruct((e, d, ff2), BF16),
        compiler_params=_params(("arbitrary", "arbitrary"), blocks, 3 * _nbytes((bk, ff2), BF16)),
        name="deinterleave_cast",
    )(w, perm)


def _fourier_matrices(n, n_sub):
    cc, sc = _dft_cos_sin(FOURIER_GROUP_W)
    w_chan = (jnp.concatenate([cc, sc], axis=1) * FOURIER_GROUP_W ** -0.5).astype(BF16)
    cn, sn = _dft_cos_sin(n)
    lat = jnp.concatenate([cn, -sn], axis=1) * n ** -0.5
    m = n // n_sub
    cm, sm = _dft_cos_sin(m)
    eye = jnp.eye(n_sub, dtype=F32)
    ctx = jnp.concatenate([jnp.kron(eye, cm), -jnp.kron(eye, sm)], axis=1) * m ** -0.5
    return w_chan, jnp.stack([lat, ctx]).astype(BF16)


def _shift_cast_kernel(a_ref, b_ref, o_ref, *, shift):
    o_ref[0] = jnp.concatenate([a_ref[0][:, shift:], b_ref[0][:, :shift]], axis=1).astype(o_ref.dtype)


def _column_window_cast(w, off, bn=512):
    depth, k, n = w.shape
    n_out = n - off
    shift = off % LANES
    base = off - shift
    assert shift > 0 and base % bn == 0 and n_out % bn == 0 and bn % LANES == 0
    bk = _pick(k, 2048)
    blocks = _nbytes((bk, bn + LANES), F32) + _nbytes((bk, bn), BF16)
    return pl.pallas_call(
        functools.partial(_shift_cast_kernel, shift=shift),
        grid=(depth, k // bk, n_out // bn),
        in_specs=[
            pl.BlockSpec((1, bk, bn), lambda l, i, j: (l, i, base // bn + j)),
            pl.BlockSpec((1, bk, LANES), lambda l, i, j: (l, i, (base + (j + 1) * bn) // LANES)),
        ],
        out_specs=pl.BlockSpec((1, bk, bn), lambda l, i, j: (l, i, j)),
        out_shape=jax.ShapeDtypeStruct((depth, k, n_out), BF16),
        compiler_params=_params(("arbitrary", "arbitrary", "arbitrary"), blocks, 2 * _nbytes((bk, bn), F32)),
        name="column_window_cast",
    )(w, w)


def _stacked_weights(w_in, w_br_gqa, w_br_mla, w_br_fourier, w_out, w_gate_up, b_gate_up, w_down, b_down):
    depth, e = w_gate_up.shape[:2]
    kv_cols = 2 * GQA_KV_W + MLA_KV_LORA + MLA_ROPE_DIM
    w_kv = jnp.pad(w_in[:, :, :kv_cols], ((0, 0), (0, 0), (0, KV_W - kv_cols))).astype(BF16)
    w_main = _column_window_cast(w_in, kv_cols)
    wgu = _deinterleave_cast(w_gate_up.reshape((depth * e,) + w_gate_up.shape[2:]))
    bgu = jnp.concatenate([b_gate_up[..., 0::2], b_gate_up[..., 1::2]], axis=-1).reshape(depth * e, -1)
    wd = w_down.astype(BF16).reshape((depth * e,) + w_down.shape[2:])
    return dict(w_kv=w_kv, w_main=w_main, w_br_gqa=w_br_gqa.astype(BF16), w_br_mla=w_br_mla.astype(BF16),
                w_br_fourier=w_br_fourier.astype(BF16), w_out=w_out.astype(BF16), wgu=wgu, bgu=bgu, wd=wd,
                bd=b_down.reshape(depth * e, -1))


def _layer_weights(l, mla_w_uq, mla_w_ukv, w_router, b_router):
    uq = mla_w_uq[l].reshape(MLA_Q_LORA, MLA_HEADS, MLA_QK_DIM)
    uq_rope = jnp.pad(uq[:, :, MLA_NOPE_DIM:], ((0, 0), (0, 0), (0, LANES - MLA_ROPE_DIM)))
    w_uq = jnp.concatenate([uq[:, :, :MLA_NOPE_DIM].reshape(MLA_Q_LORA, -1), uq_rope.reshape(MLA_Q_LORA, -1)],
                           axis=1).astype(BF16)
    ukv = mla_w_ukv[l].reshape(MLA_KV_LORA, MLA_HEADS, MLA_NOPE_DIM + MLA_V_DIM)
    w_ukv = jnp.concatenate([ukv[:, :, :MLA_NOPE_DIM].reshape(MLA_KV_LORA, -1),
                             ukv[:, :, MLA_NOPE_DIM:].reshape(MLA_KV_LORA, -1)], axis=1).astype(BF16)
    e = w_router.shape[2]
    wr = jnp.pad(w_router[l], ((0, 0), (0, LANES - e))).astype(BF16)
    br = jnp.pad(b_router[l], (0, LANES - e), constant_values=NEG_BIG).reshape(1, LANES)
    return dict(w_uq=w_uq, w_ukv=w_ukv, wr=wr, br=br, n_experts=e)


def _route(idx, n_experts, bm):
    pairs = idx.shape[0] * TOP_K
    flat_e = idx.reshape(pairs)
    onehot = (flat_e[:, None] == jnp.arange(n_experts, dtype=jnp.int32)[None, :]).astype(jnp.int32)
    csum = jnp.cumsum(onehot, axis=0)
    counts = csum[-1]
    tiles_e = (counts + bm - 1) // bm
    tile_end = jnp.cumsum(tiles_e)
    tile_start = tile_end - tiles_e
    dest = jnp.sum(onehot * (csum - 1 + tile_start[None, :] * bm), axis=1)
    n_tiles = pairs // bm + n_experts
    rows = n_tiles * bm
    row_token = jnp.zeros((rows,), jnp.int32).at[dest].set(jnp.arange(pairs, dtype=jnp.int32) // TOP_K)
    tile_ids = jnp.arange(n_tiles, dtype=jnp.int32)
    tile_expert = jnp.minimum(jnp.sum((tile_ids[:, None] >= tile_end[None, :]).astype(jnp.int32), axis=1),
                              n_experts - 1)
    tile_valid = (tile_ids < tile_end[-1]).astype(jnp.int32)
    return row_token, dest, tile_expert, tile_valid


def _layer(h_parts, u, mod3, l, sw, lw, gains, tabs, fmats, n_lat, ctx_len, tail):
    last = tail[0] == "final"
    nb_all = sum(p.shape[0] for p in h_parts)
    _, n, d = h_parts[0].shape
    nb = n_lat if last else nb_all
    rows = nb * n
    cg, sg, cm, sm = tabs
    w_chan, w_pos = fmats

    if u is None:
        u = _norm_mod(h_parts, gains["norm_mix"], mod3, 0, 1)
    u2 = u.reshape(nb_all * n, d)
    z_kv = _matmul(u2, sw["w_kv"], l, nb_all * n).reshape(nb_all, n, KV_W)
    z_main2 = _matmul(u2, sw["w_main"], l, rows)
    z_main = z_main2.reshape(nb, n, -1)
    ka, kvb, kr = _prep_kv(z_kv, gains["gqa_k_norm"], gains["mla_kv_norm"], lw["w_ukv"], (cg, sg, cm, sm), n_lat)
    qa, qb = _prep_q(z_main, nb, gains["gqa_q_norm"], gains["mla_q_norm"], lw["w_uq"], (cg, sg, cm, sm), n_lat)
    y_a = _gqa_attention(qa, ka, z_kv, nb, n_lat, ctx_len)
    y_b = _mla_attention(qb, kvb, kr, nb, n_lat, ctx_len)
    y_c = _fourier(z_main, nb, n_lat, w_chan, w_pos)
    m = _merge(y_a.reshape(rows, -1), y_b.reshape(rows, -1), y_c.reshape(rows, -1),
               sw["w_br_gqa"], sw["w_br_mla"], sw["w_br_fourier"], l, z_main2, rows, d)
    h2 = _out_proj(m, sw["w_out"], l, tuple(p.reshape(-1, d) for p in h_parts), mod3, 2, rows, n)

    v_packed, idx, wts = _ffn_pre(h2.reshape(nb, n, d), nb, gains["norm_ffn"], mod3, 3, 4, lw["wr"], lw["br"])
    n_experts = lw["n_experts"]
    bm = 256
    row_token, dest, tile_expert, tile_valid = _route(idx.reshape(rows, LANES)[:, :TOP_K], n_experts, bm)
    y_rows = _experts(v_packed, row_token, tile_expert, tile_valid, sw["wgu"], sw["bgu"], sw["wd"], sw["bd"],
                      l * n_experts, bm)
    wts2 = wts.reshape(rows, LANES)
    if last:
        out = _combine(y_rows, dest, wts2, h2, mod3, 5, rows, n, "final", tail[1], mod3, 0, 1)
        return out.reshape(nb, n, d), None
    out, u_next = _combine(y_rows, dest, wts2, h2, mod3, 5, rows, n, "next", tail[1], tail[2], 0, 1)
    return out.reshape(nb, n, d), u_next.reshape(nb, n, d)


def kernel(x, c, ctx, c_ctx, w_ada, b_ada, norm_mix, w_in, gqa_q_norm, gqa_k_norm, mla_q_norm, mla_w_uq, mla_kv_norm, mla_w_ukv, w_br_gqa, w_br_mla, w_br_fourier, w_out, norm_ffn, w_router, b_router, w_gate_up, b_gate_up, w_down, b_down, norm_final):
    n_lat, n, d = x.shape
    ctx_len = ctx.shape[1]
    depth = w_ada.shape[0]
    assert n_lat * ctx_len == n, "context tokens must fill exactly one extra batch row"
    assert n % GRID_W == 0 and ctx_len % 8 == 0

    tabs_g = _rope_tables(n, HEAD_DIM)
    tabs_m = _rope_tables(n, MLA_ROPE_DIM)
    tabs = (tabs_g[0], tabs_g[1], tabs_m[0], tabs_m[1])
    fmats = _fourier_matrices(n, n_lat)

    mod_rows = 16
    c_all = jnp.zeros((mod_rows, d), F32).at[:n_lat].set(c).at[n_lat].set(c_ctx)
    mod = _ada_mod(c_all, w_ada, b_ada)

    sw = _stacked_weights(w_in, w_br_gqa, w_br_mla, w_br_fourier, w_out, w_gate_up, b_gate_up, w_down, b_down)
    mod3 = [mod[l].reshape(mod_rows, 1, 6 * d) for l in range(depth)]
    h_parts = (x, ctx.reshape(1, n, d))
    u = None
    for l in range(depth):
        lw = _layer_weights(l, mla_w_uq, mla_w_ukv, w_router, b_router)
        gains = dict(norm_mix=norm_mix[l], gqa_q_norm=gqa_q_norm[l], gqa_k_norm=gqa_k_norm[l],
                     mla_q_norm=mla_q_norm[l], mla_kv_norm=mla_kv_norm[l], norm_ffn=norm_ffn[l])
        tail = ("final", norm_final) if l == depth - 1 else ("next", norm_mix[l + 1], mod3[l + 1])
        h, u = _layer(h_parts, u, mod3[l], l, sw, lw, gains, tabs, fmats, n_lat, ctx_len, tail)
        h_parts = (h,)
    return h_parts[0]
```
